```python
import jax, jax.numpy as jnp
from jax import lax
import numpy as np

D_MODEL = 1024
BATCH = 8
SEQ = 8192
DEPTH = 4

CHUNK = 64
EPS = 1e-6
N_BRANCH = 2
CONV_DIM = D_MODEL
CONV_WIDTH = 3
SSM_EXPAND = 2
D_SSM = SSM_EXPAND * D_MODEL
SSM_HEAD_DIM = 64
SSM_HEADS = D_SSM // SSM_HEAD_DIM
SSM_GROUPS = 8
SSM_STATE = 128
SSM_CONV_WIDTH = 4
SSM_CONV_DIM = D_SSM + 2 * SSM_GROUPS * SSM_STATE
DT_MIN = 1e-3
DT_MAX = 1e-1
D_FF = 4 * D_MODEL
N_MOD = 6

kernel_name = "hybrid_shortconv_ssd_gated_trunk"


def proj_sizes():
    return (N_BRANCH * D_MODEL,
            CONV_DIM, CONV_DIM, CONV_DIM,
            D_SSM,
            SSM_CONV_DIM,
            SSM_HEADS)


def rms_norm(x, w):
    xf = x.astype(jnp.float32)
    y = xf * lax.rsqrt(jnp.mean(xf * xf, axis=-1, keepdims=True) + EPS)
    return (y * w.astype(jnp.float32)).astype(x.dtype)


def causal_depthwise_conv(x, w):
    k = w.shape[0]
    return lax.conv_general_dilated(
        x, w[:, None, :].astype(x.dtype), window_strides=(1,), padding=((k - 1, 0),),
        dimension_numbers=('NWC', 'WIO', 'NWC'), feature_group_count=x.shape[-1])


def ssd_scan(x, a, b, c):
    bsz, seqlen, h, p = x.shape
    g, n = b.shape[-2:]
    r = h // g
    nc = seqlen // CHUNK

    def to_chunks(t):
        return jnp.moveaxis(t.reshape(bsz, nc, CHUNK, *t.shape[2:]), 1, 0)

    xs = (to_chunks(x.reshape(bsz, seqlen, g, r, p)),
          to_chunks(a.reshape(bsz, seqlen, g, r)),
          to_chunks(b), to_chunks(c))
    tril = jnp.tril(jnp.ones((CHUNK, CHUNK), dtype=bool))[None, :, :, None, None]

    def step(state, inp):
        xq, aq, bq, cq = inp
        a_cum = jnp.cumsum(aq, axis=1)
        seg = a_cum[:, :, None] - a_cum[:, None, :]
        decay = jnp.exp(jnp.where(tril, seg, -jnp.inf))
        scores = jnp.einsum('btgn,bsgn->btsg', cq, bq)
        y_diag = jnp.einsum('btsg,btsgr,bsgrp->btgrp', scores, decay, xq)
        y_off = jnp.einsum('btgn,bgrpn->btgrp', cq, state) * jnp.exp(a_cum)[..., None]
        to_end = jnp.exp(a_cum[:, -1:] - a_cum)
        new_state = (state * jnp.exp(a_cum[:, -1])[..., None, None]
                     + jnp.einsum('bsgn,bsgr,bsgrp->bgrpn', bq, to_end, xq))
        return new_state, y_diag + y_off

    state0 = jnp.zeros((bsz, g, r, p, n), jnp.float32)
    _, y = lax.scan(step, state0, xs)
    return jnp.moveaxis(y, 0, 1).reshape(bsz, seqlen, h, p)


def mixer_sublayer(u, w_in, conv_w, ssm_conv_w, ssm_conv_b, dt_bias, a_log, d_skip,
                   ssm_norm_w, w_conv_out, w_ssm_out, w_o):
    bsz, seqlen, _ = u.shape
    proj = u @ w_in
    split_at = [int(v) for v in np.cumsum(proj_sizes())[:-1]]
    gl, cb, cc, cx, z, xbc, dt = jnp.split(proj, split_at, axis=-1)

    y_conv = cb * causal_depthwise_conv(cc * cx, conv_w)
    p_conv = y_conv @ w_conv_out

    xbc = jax.nn.silu(causal_depthwise_conv(xbc, ssm_conv_w) + ssm_conv_b.astype(xbc.dtype))
    xbc = xbc.astype(jnp.float32)
    xs, bs, cs = jnp.split(xbc, [D_SSM, D_SSM + SSM_GROUPS * SSM_STATE], axis=-1)
    xs = xs.reshape(bsz, seqlen, SSM_HEADS, SSM_HEAD_DIM)
    bs = bs.reshape(bsz, seqlen, SSM_GROUPS, SSM_STATE)
    cs = cs.reshape(bsz, seqlen, SSM_GROUPS, SSM_STATE)
    dt = jax.nn.softplus(dt.astype(jnp.float32) + dt_bias.astype(jnp.float32))
    a = -jnp.exp(a_log.astype(jnp.float32))
    y = ssd_scan(xs * dt[..., None], dt * a, bs, cs)
    y = y + d_skip.astype(jnp.float32)[:, None] * xs
    y = y.reshape(bsz, seqlen, D_SSM) * jax.nn.silu(z.astype(jnp.float32))
    yg = y.reshape(bsz, seqlen, SSM_GROUPS, D_SSM // SSM_GROUPS)
    yg = yg * lax.rsqrt(jnp.mean(yg * yg, axis=-1, keepdims=True) + EPS)
    y = (yg.reshape(bsz, seqlen, D_SSM) * ssm_norm_w.astype(jnp.float32)).astype(u.dtype)
    p_ssm = y @ w_ssm_out

    g_conv, g_ssm = jnp.split(jax.nn.sigmoid(gl), 2, axis=-1)
    merged = g_conv * p_conv + g_ssm * p_ssm
    return merged @ w_o


def _fwd_setup_inputs(seed: int = 0) -> dict:
    key = jax.random.key(seed)
    ks = jax.random.split(key, 24)
    d_proj = sum(proj_sizes())
    f32 = jnp.float32

    def nrm(k, shape, scale):
        return jax.random.normal(k, shape, f32) * scale

    dt0 = jnp.exp(jax.random.uniform(ks[10], (DEPTH, SSM_HEADS), f32,
                                     np.log(DT_MIN), np.log(DT_MAX)))
    dt_bias = dt0 + jnp.log(-jnp.expm1(-dt0))
    return {
        "x": nrm(ks[0], (BATCH, SEQ, D_MODEL), 1.0),
        "c": nrm(ks[1], (BATCH, D_MODEL), 1.0),
        "w_ada": nrm(ks[2], (DEPTH, D_MODEL, N_MOD * D_MODEL), 0.5 * D_MODEL ** -0.5),
        "b_ada": nrm(ks[3], (DEPTH, N_MOD * D_MODEL), 0.02),
        "ln1": 1.0 + nrm(ks[4], (DEPTH, D_MODEL), 0.1),
        "ln2": 1.0 + nrm(ks[5], (DEPTH, D_MODEL), 0.1),
        "w_in": nrm(ks[6], (DEPTH, D_MODEL, d_proj), D_MODEL ** -0.5),
        "conv_w": nrm(ks[7], (DEPTH, CONV_WIDTH, CONV_DIM), CONV_WIDTH ** -0.5),
        "ssm_conv_w": nrm(ks[8], (DEPTH, SSM_CONV_WIDTH, SSM_CONV_DIM), SSM_CONV_WIDTH ** -0.5),
        "ssm_conv_b": nrm(ks[9], (DEPTH, SSM_CONV_DIM), 0.01),
        "dt_bias": dt_bias,
        "a_log": jnp.log(jax.random.uniform(ks[11], (DEPTH, SSM_HEADS), f32, 1.0, 16.0)),
        "d_skip": 1.0 + nrm(ks[12], (DEPTH, SSM_HEADS), 0.1),
        "ssm_norm_w": 1.0 + nrm(ks[13], (DEPTH, D_SSM), 0.1),
        "w_conv_out": nrm(ks[14], (DEPTH, CONV_DIM, D_MODEL), CONV_DIM ** -0.5),
        "w_ssm_out": nrm(ks[15], (DEPTH, D_SSM, D_MODEL), D_SSM ** -0.5),
        "w_o": nrm(ks[16], (DEPTH, D_MODEL, D_MODEL), D_MODEL ** -0.5),
        "w_up": nrm(ks[17], (DEPTH, D_MODEL, D_FF), D_MODEL ** -0.5),
        "w_down": nrm(ks[18], (DEPTH, D_FF, D_MODEL), D_FF ** -0.5),
        "final_norm": 1.0 + nrm(ks[19], (D_MODEL,), 0.1),
    }


def _fwd_reference(x, c, w_ada, b_ada, ln1, ln2, w_in, conv_w, ssm_conv_w, ssm_conv_b,
              dt_bias, a_log, d_skip, ssm_norm_w, w_conv_out, w_ssm_out, w_o,
              w_up, w_down, final_norm):
    bsz = x.shape[0]
    c_act = jax.nn.silu(c)
    for i in range(DEPTH):
        mod = (c_act @ w_ada[i] + b_ada[i]).reshape(bsz, N_MOD, D_MODEL)[:, :, None, :]
        shift1, scale1, gate1 = mod[:, 0], mod[:, 1], mod[:, 2]
        shift2, scale2, gate2 = mod[:, 3], mod[:, 4], mod[:, 5]

        u = rms_norm(x, ln1[i]) * (1.0 + scale1) + shift1
        mix = mixer_sublayer(u, w_in[i], conv_w[i], ssm_conv_w[i], ssm_conv_b[i], dt_bias[i],
                             a_log[i], d_skip[i], ssm_norm_w[i], w_conv_out[i],
                             w_ssm_out[i], w_o[i])
        x = x + gate1 * mix

        u2 = rms_norm(x, ln2[i]) * (1.0 + scale2) + shift2
        hid = jnp.square(jax.nn.relu(u2 @ w_up[i]))
        x = x + gate2 * (hid @ w_down[i])
    return rms_norm(x, final_norm)


import jax as _jax
import jax.numpy as _jnp

TWIN_FORMAT = 'train_step'
FWD_PARAMS = ['x', 'c', 'w_ada', 'b_ada', 'ln1', 'ln2', 'w_in', 'conv_w', 'ssm_conv_w', 'ssm_conv_b', 'dt_bias', 'a_log', 'd_skip', 'ssm_norm_w', 'w_conv_out', 'w_ssm_out', 'w_o', 'w_up', 'w_down', 'final_norm']
TWIN_WEIGHTS = ['w_ada', 'b_ada', 'ln1', 'ln2', 'w_in', 'conv_w', 'ssm_conv_w', 'ssm_conv_b', 'dt_bias', 'a_log', 'd_skip', 'ssm_norm_w', 'w_conv_out', 'w_ssm_out', 'w_o', 'w_up', 'w_down', 'final_norm']
TWIN_DIFF_INPUT = 'x'
TWIN_INPUTS = ['x', 'c', 'w_ada', 'b_ada', 'ln1', 'ln2', 'w_in', 'conv_w', 'ssm_conv_w', 'ssm_conv_b', 'dt_bias', 'a_log', 'd_skip', 'ssm_norm_w', 'w_conv_out', 'w_ssm_out', 'w_o', 'w_up', 'w_down', 'final_norm', 'loss_target', 'm_w_ada', 'm_b_ada', 'm_ln1', 'm_ln2', 'm_w_in', 'm_conv_w', 'm_ssm_conv_w', 'm_ssm_conv_b', 'm_dt_bias', 'm_a_log', 'm_d_skip', 'm_ssm_norm_w', 'm_w_conv_out', 'm_w_ssm_out', 'm_w_o', 'm_w_up', 'm_w_down', 'm_final_norm', 'v_w_ada', 'v_b_ada', 'v_ln1', 'v_ln2', 'v_w_in', 'v_conv_w', 'v_ssm_conv_w', 'v_ssm_conv_b', 'v_dt_bias', 'v_a_log', 'v_d_skip', 'v_ssm_norm_w', 'v_w_conv_out', 'v_w_ssm_out', 'v_w_o', 'v_w_up', 'v_w_down', 'v_final_norm']
TWIN_OUTPUTS = ['loss', 'grad_x', 'grad_w_ada', 'grad_b_ada', 'grad_ln1', 'grad_ln2', 'grad_w_in', 'grad_conv_w', 'grad_ssm_conv_w', 'grad_ssm_conv_b', 'grad_dt_bias', 'grad_a_log', 'grad_d_skip', 'grad_ssm_norm_w', 'grad_w_conv_out', 'grad_w_ssm_out', 'grad_w_o', 'grad_w_up', 'grad_w_down', 'grad_final_norm', 'delta_w_ada', 'delta_b_ada', 'delta_ln1', 'delta_ln2', 'delta_w_in', 'delta_conv_w', 'delta_ssm_conv_w', 'delta_ssm_conv_b', 'delta_dt_bias', 'delta_a_log', 'delta_d_skip', 'delta_ssm_norm_w', 'delta_w_conv_out', 'delta_w_ssm_out', 'delta_w_o', 'delta_w_up', 'delta_w_down', 'delta_final_norm', 'new_m_w_ada', 'new_m_b_ada', 'new_m_ln1', 'new_m_ln2', 'new_m_w_in', 'new_m_conv_w', 'new_m_ssm_conv_w', 'new_m_ssm_conv_b', 'new_m_dt_bias', 'new_m_a_log', 'new_m_d_skip', 'new_m_ssm_norm_w', 'new_m_w_conv_out', 'new_m_w_ssm_out', 'new_m_w_o', 'new_m_w_up', 'new_m_w_down', 'new_m_final_norm', 'new_v_w_ada', 'new_v_b_ada', 'new_v_ln1', 'new_v_ln2', 'new_v_w_in', 'new_v_conv_w', 'new_v_ssm_conv_w', 'new_v_ssm_conv_b', 'new_v_dt_bias', 'new_v_a_log', 'new_v_d_skip', 'new_v_ssm_norm_w', 'new_v_w_conv_out', 'new_v_w_ssm_out', 'new_v_w_o', 'new_v_w_up', 'new_v_w_down', 'new_v_final_norm']
TWIN_LEAF_KINDS = {'loss': 'loss', 'grad_x': 'grad_x', 'grad_w_ada': 'grad_w', 'grad_b_ada': 'grad_w', 'grad_ln1': 'grad_w', 'grad_ln2': 'grad_w', 'grad_w_in': 'grad_w', 'grad_conv_w': 'grad_w', 'grad_ssm_conv_w': 'grad_w', 'grad_ssm_conv_b': 'grad_w', 'grad_dt_bias': 'grad_w', 'grad_a_log': 'grad_w', 'grad_d_skip': 'grad_w', 'grad_ssm_norm_w': 'grad_w', 'grad_w_conv_out': 'grad_w', 'grad_w_ssm_out': 'grad_w', 'grad_w_o': 'grad_w', 'grad_w_up': 'grad_w', 'grad_w_down': 'grad_w', 'grad_final_norm': 'grad_w', 'delta_w_ada': 'delta_w', 'delta_b_ada': 'delta_w', 'delta_ln1': 'delta_w', 'delta_ln2': 'delta_w', 'delta_w_in': 'delta_w', 'delta_conv_w': 'delta_w', 'delta_ssm_conv_w': 'delta_w', 'delta_ssm_conv_b': 'delta_w', 'delta_dt_bias': 'delta_w', 'delta_a_log': 'delta_w', 'delta_d_skip': 'delta_w', 'delta_ssm_norm_w': 'delta_w', 'delta_w_conv_out': 'delta_w', 'delta_w_ssm_out': 'delta_w', 'delta_w_o': 'delta_w', 'delta_w_up': 'delta_w', 'delta_w_down': 'delta_w', 'delta_final_norm': 'delta_w', 'new_m_w_ada': 'new_m', 'new_m_b_ada': 'new_m', 'new_m_ln1': 'new_m', 'new_m_ln2': 'new_m', 'new_m_w_in': 'new_m', 'new_m_conv_w': 'new_m', 'new_m_ssm_conv_w': 'new_m', 'new_m_ssm_conv_b': 'new_m', 'new_m_dt_bias': 'new_m', 'new_m_a_log': 'new_m', 'new_m_d_skip': 'new_m', 'new_m_ssm_norm_w': 'new_m', 'new_m_w_conv_out': 'new_m', 'new_m_w_ssm_out': 'new_m', 'new_m_w_o': 'new_m', 'new_m_w_up': 'new_m', 'new_m_w_down': 'new_m', 'new_m_final_norm': 'new_m', 'new_v_w_ada': 'new_v', 'new_v_b_ada': 'new_v', 'new_v_ln1': 'new_v', 'new_v_ln2': 'new_v', 'new_v_w_in': 'new_v', 'new_v_conv_w': 'new_v', 'new_v_ssm_conv_w': 'new_v', 'new_v_ssm_conv_b': 'new_v', 'new_v_dt_bias': 'new_v', 'new_v_a_log': 'new_v', 'new_v_d_skip': 'new_v', 'new_v_ssm_norm_w': 'new_v', 'new_v_w_conv_out': 'new_v', 'new_v_w_ssm_out': 'new_v', 'new_v_w_o': 'new_v', 'new_v_w_up': 'new_v', 'new_v_w_down': 'new_v', 'new_v_final_norm': 'new_v'}


def _forward(args):
    return _fwd_reference(*[args[k] for k in FWD_PARAMS])


def _output_shape():
    def fwd():
        inp = _fwd_setup_inputs(0)
        return _fwd_reference(*[inp[k] for k in FWD_PARAMS])
    out = _jax.eval_shape(fwd)
    return out.shape, out.dtype

N_MICROBATCH = 1
ADAM_LR = 0.001
ADAM_B1 = 0.9
ADAM_B2 = 0.999
ADAM_EPS = 1e-08
ADAM_WD = 0.01
ADAM_STEP = 10
PER_EXAMPLE_BATCH_AXIS = {'x': 0, 'c': 0, 'loss_target': 0}
SHARED_INPUTS = []
_WEIGHT_DTYPES = {'w_ada': _jnp.float32, 'b_ada': _jnp.float32, 'ln1': _jnp.float32, 'ln2': _jnp.float32, 'w_in': _jnp.float32, 'conv_w': _jnp.float32, 'ssm_conv_w': _jnp.float32, 'ssm_conv_b': _jnp.float32, 'dt_bias': _jnp.float32, 'a_log': _jnp.float32, 'd_skip': _jnp.float32, 'ssm_norm_w': _jnp.float32, 'w_conv_out': _jnp.float32, 'w_ssm_out': _jnp.float32, 'w_o': _jnp.float32, 'w_up': _jnp.float32, 'w_down': _jnp.float32, 'final_norm': _jnp.float32}
MOMENT_SCALE = {'w_ada': 3.178154e-01, 'b_ada': 7.225870e-01, 'ln1': 1.040601e-01, 'ln2': 1.227056e-01, 'w_in': 3.382497e-02, 'conv_w': 5.314255e-02, 'ssm_conv_w': 2.397686e-02, 'ssm_conv_b': 3.898002e-02, 'dt_bias': 5.673990e-02, 'a_log': 1.100332e-01, 'd_skip': 1.355268e-01, 'ssm_norm_w': 3.513869e-02, 'w_conv_out': 5.149784e-02, 'w_ssm_out': 4.827632e-02, 'w_o': 7.100142e-02, 'w_up': 6.651968e-02, 'w_down': 2.172912e-01, 'final_norm': 6.468031e+01}


def _to_microbatches(a, axis):
    t = _jnp.moveaxis(a, axis, 0)
    t = t.reshape((N_MICROBATCH, t.shape[0] // N_MICROBATCH) + t.shape[1:])
    return _jnp.moveaxis(t, 1, axis + 1)


def setup_inputs(seed: int = 0) -> dict:
    inp = _fwd_setup_inputs(seed)
    key = _jax.random.fold_in(_jax.random.key(seed), 7919)
    shape, _ = _output_shape()
    out = dict(inp)
    out["loss_target"] = _jax.random.normal(_jax.random.fold_in(key, 0), shape, _jnp.float32)
    for i, name in enumerate(TWIN_WEIGHTS):
        w = inp[name].astype(_jnp.float32)
        if MOMENT_SCALE is None:
            s = _jnp.sqrt(_jnp.mean(_jnp.square(w)) + 1e-30)
        else:
            s = MOMENT_SCALE[name]
        km, kv = _jax.random.split(_jax.random.fold_in(key, i + 1))
        out[name] = w
        out["m_" + name] = s * _jax.random.normal(km, w.shape, _jnp.float32)
        out["v_" + name] = (s * s) * _jax.random.uniform(kv, w.shape, _jnp.float32, 0.5, 1.5)
    if N_MICROBATCH > 1:
        for name, axis in PER_EXAMPLE_BATCH_AXIS.items():
            out[name] = _to_microbatches(out[name], axis)
    return {'x': out['x'], 'c': out['c'], 'w_ada': out['w_ada'], 'b_ada': out['b_ada'], 'ln1': out['ln1'], 'ln2': out['ln2'], 'w_in': out['w_in'], 'conv_w': out['conv_w'], 'ssm_conv_w': out['ssm_conv_w'], 'ssm_conv_b': out['ssm_conv_b'], 'dt_bias': out['dt_bias'], 'a_log': out['a_log'], 'd_skip': out['d_skip'], 'ssm_norm_w': out['ssm_norm_w'], 'w_conv_out': out['w_conv_out'], 'w_ssm_out': out['w_ssm_out'], 'w_o': out['w_o'], 'w_up': out['w_up'], 'w_down': out['w_down'], 'final_norm': out['final_norm'], 'loss_target': out['loss_target'], 'm_w_ada': out['m_w_ada'], 'm_b_ada': out['m_b_ada'], 'm_ln1': out['m_ln1'], 'm_ln2': out['m_ln2'], 'm_w_in': out['m_w_in'], 'm_conv_w': out['m_conv_w'], 'm_ssm_conv_w': out['m_ssm_conv_w'], 'm_ssm_conv_b': out['m_ssm_conv_b'], 'm_dt_bias': out['m_dt_bias'], 'm_a_log': out['m_a_log'], 'm_d_skip': out['m_d_skip'], 'm_ssm_norm_w': out['m_ssm_norm_w'], 'm_w_conv_out': out['m_w_conv_out'], 'm_w_ssm_out': out['m_w_ssm_out'], 'm_w_o': out['m_w_o'], 'm_w_up': out['m_w_up'], 'm_w_down': out['m_w_down'], 'm_final_norm': out['m_final_norm'], 'v_w_ada': out['v_w_ada'], 'v_b_ada': out['v_b_ada'], 'v_ln1': out['v_ln1'], 'v_ln2': out['v_ln2'], 'v_w_in': out['v_w_in'], 'v_conv_w': out['v_conv_w'], 'v_ssm_conv_w': out['v_ssm_conv_w'], 'v_ssm_conv_b': out['v_ssm_conv_b'], 'v_dt_bias': out['v_dt_bias'], 'v_a_log': out['v_a_log'], 'v_d_skip': out['v_d_skip'], 'v_ssm_norm_w': out['v_ssm_norm_w'], 'v_w_conv_out': out['v_w_conv_out'], 'v_w_ssm_out': out['v_w_ssm_out'], 'v_w_o': out['v_w_o'], 'v_w_up': out['v_w_up'], 'v_w_down': out['v_w_down'], 'v_final_norm': out['v_final_norm']}


def _loss(weights, diff, rest, loss_target):
    with _jax.named_scope("forward"):
        args = {**rest, TWIN_DIFF_INPUT: diff, **{k: w.astype(_WEIGHT_DTYPES[k]) for k, w in weights.items()}}
        y = _forward(args)
    with _jax.named_scope("loss_head"):
        err = _jnp.square(y.astype(_jnp.float32) - loss_target)
        return 0.5 * _jnp.sum(_jnp.mean(err, axis=-1)) if err.ndim else 0.5 * err


def _adamw(w, g, m, v):
    m = ADAM_B1 * m + (1.0 - ADAM_B1) * g
    v = ADAM_B2 * v + (1.0 - ADAM_B2) * _jnp.square(g)
    m_hat = m / (1.0 - ADAM_B1 ** ADAM_STEP)
    v_hat = v / (1.0 - ADAM_B2 ** ADAM_STEP)
    delta = -ADAM_LR * (m_hat / (_jnp.sqrt(v_hat) + ADAM_EPS) + ADAM_WD * w)
    return delta, m, v


def reference(x, c, w_ada, b_ada, ln1, ln2, w_in, conv_w, ssm_conv_w, ssm_conv_b, dt_bias, a_log, d_skip, ssm_norm_w, w_conv_out, w_ssm_out, w_o, w_up, w_down, final_norm, loss_target, m_w_ada, m_b_ada, m_ln1, m_ln2, m_w_in, m_conv_w, m_ssm_conv_w, m_ssm_conv_b, m_dt_bias, m_a_log, m_d_skip, m_ssm_norm_w, m_w_conv_out, m_w_ssm_out, m_w_o, m_w_up, m_w_down, m_final_norm, v_w_ada, v_b_ada, v_ln1, v_ln2, v_w_in, v_conv_w, v_ssm_conv_w, v_ssm_conv_b, v_dt_bias, v_a_log, v_d_skip, v_ssm_norm_w, v_w_conv_out, v_w_ssm_out, v_w_o, v_w_up, v_w_down, v_final_norm):
    given = dict(x=x, c=c, w_ada=w_ada, b_ada=b_ada, ln1=ln1, ln2=ln2, w_in=w_in, conv_w=conv_w, ssm_conv_w=ssm_conv_w, ssm_conv_b=ssm_conv_b, dt_bias=dt_bias, a_log=a_log, d_skip=d_skip, ssm_norm_w=ssm_norm_w, w_conv_out=w_conv_out, w_ssm_out=w_ssm_out, w_o=w_o, w_up=w_up, w_down=w_down, final_norm=final_norm, loss_target=loss_target, m_w_ada=m_w_ada, m_b_ada=m_b_ada, m_ln1=m_ln1, m_ln2=m_ln2, m_w_in=m_w_in, m_conv_w=m_conv_w, m_ssm_conv_w=m_ssm_conv_w, m_ssm_conv_b=m_ssm_conv_b, m_dt_bias=m_dt_bias, m_a_log=m_a_log, m_d_skip=m_d_skip, m_ssm_norm_w=m_ssm_norm_w, m_w_conv_out=m_w_conv_out, m_w_ssm_out=m_w_ssm_out, m_w_o=m_w_o, m_w_up=m_w_up, m_w_down=m_w_down, m_final_norm=m_final_norm, v_w_ada=v_w_ada, v_b_ada=v_b_ada, v_ln1=v_ln1, v_ln2=v_ln2, v_w_in=v_w_in, v_conv_w=v_conv_w, v_ssm_conv_w=v_ssm_conv_w, v_ssm_conv_b=v_ssm_conv_b, v_dt_bias=v_dt_bias, v_a_log=v_a_log, v_d_skip=v_d_skip, v_ssm_norm_w=v_ssm_norm_w, v_w_conv_out=v_w_conv_out, v_w_ssm_out=v_w_ssm_out, v_w_o=v_w_o, v_w_up=v_w_up, v_w_down=v_w_down, v_final_norm=v_final_norm)
    weights = {n: given[n] for n in TWIN_WEIGHTS}
    shared = {n: given[n] for n in SHARED_INPUTS}
    per_example = {n: given[n] for n in ['x', 'c']}
    grad_fn = _jax.value_and_grad(_loss, argnums=(0, 1))

    def one_microbatch(ex, loss_target):
        ex = dict(ex)
        diff = ex.pop(TWIN_DIFF_INPUT)
        return grad_fn(weights, diff, {**shared, **ex}, loss_target)

    if N_MICROBATCH == 1:
        loss, (grad_w, grad_x) = one_microbatch(per_example, given["loss_target"])
    else:
        def body(carry, xs):
            loss_sum, grad_sum = carry
            l_k, (gw_k, gx_k) = one_microbatch(xs[0], xs[1])
            with _jax.named_scope("update"):
                return (loss_sum + l_k, _jax.tree.map(_jnp.add, grad_sum, gw_k)), gx_k

        init = (_jnp.zeros((), _jnp.float32), _jax.tree.map(_jnp.zeros_like, weights))
        (loss, grad_w), grad_x = _jax.lax.scan(body, init, (per_example, given["loss_target"]))
    with _jax.named_scope("update"):
        delta_w, new_m, new_v = {}, {}, {}
        for n in TWIN_WEIGHTS:
            delta_w[n], new_m[n], new_v[n] = _adamw(weights[n], grad_w[n], given["m_" + n], given["v_" + n])
    return (loss, grad_x, *[grad_w[n] for n in TWIN_WEIGHTS], *[delta_w[n] for n in TWIN_WEIGHTS],
            *[new_m[n] for n in TWIN_WEIGHTS], *[new_v[n] for n in TWIN_WEIGHTS])
```

```python
import jax
import jax.numpy as jnp
from jax import lax
from jax.experimental import pallas as pl
from jax.experimental.pallas import tpu as pltpu

F32, BF16 = jnp.float32, jnp.bfloat16
EPS = 1e-6
D = 1024
D_SSM = 2048
HEADS = 32
HEAD_DIM = 64
GROUPS = 8
STATE = 128
GW = D_SSM // GROUPS
XBC = 4096
N_PROJ = 11296
N_MAIN = 11264
N_PAD = 12288
DEPTH = 4
Q = 128
CBLK = 1024
ORIG_BLOCK_ORDER = (2, 3, 4, 7, 0, 1, 5, 6, 8, 9, 10)
XBC_BLOCKS = (3, 8, 9, 10)
Z_BLOCK0 = 6
GL_BLOCK2K = 2
DT_BLOCK = 11
VMEM_LIMIT_BYTES = 56 * 1024 * 1024
ADAM_LR, ADAM_B1, ADAM_B2, ADAM_EPS, ADAM_WD, ADAM_STEP = 0.001, 0.9, 0.999, 1e-08, 0.01, 10
MESH = pl.DeviceIdType.MESH


def _cparams(*sem):
    return pltpu.CompilerParams(dimension_semantics=sem, vmem_limit_bytes=VMEM_LIMIT_BYTES)


def _sigmoid(x):
    return 1.0 / (1.0 + jnp.exp(-x))


def _silu(x):
    return x * _sigmoid(x)


def _dsilu(x):
    s = _sigmoid(x)
    return s * (1.0 + x * (1.0 - s))


def _softplus(x):
    return jnp.maximum(x, 0.0) + jnp.log(1.0 + jnp.exp(-jnp.abs(x)))


def _dot(a, b, dims, prec=None):
    return lax.dot_general(a, b, (dims, ((), ())), preferred_element_type=F32, precision=prec)


NN = ((1,), (0,))
NT = ((1,), (1,))
TN = ((0,), (0,))


def _split3(x):
    hi = x.astype(BF16)
    r1 = x - hi.astype(F32)
    mid = r1.astype(BF16)
    lo = (r1 - mid.astype(F32)).astype(BF16)
    return hi, mid, lo


def _dot_exact_lhs01(m01, x):
    hi, mid, lo = _split3(x)
    return _dot(m01, hi, NN) + _dot(m01, mid, NN) + _dot(m01, lo, NN)


def _mm(a, b, mode, *, out_dtype=F32, tm=1024, tn=1024, tk=1024, name, epi=None, extra=None, prec=None,
        b_cm=False, out_cm=False, n_out=None, b_col0=0, side=None):
    if mode == "nn":
        (M, K), (K2, N) = a.shape, ((b.shape[1], b.shape[0] * b.shape[2]) if b_cm else b.shape)
    elif mode == "nt":
        (M, K), (N, K2) = a.shape, ((b.shape[1], b.shape[0] * b.shape[2]) if b_cm else b.shape)
    else:
        (K, M), (K2, N) = a.shape, b.shape
    assert K == K2, (a.shape, b.shape, mode)
    if n_out is not None:
        N = n_out
    tm, tn, tk = min(tm, M), min(tn, N), min(tk, K)
    assert M % tm == 0 and N % tn == 0 and K % tk == 0, (M, N, K, tm, tn, tk)
    nk = K // tk
    if mode == "tn":
        a_spec = pl.BlockSpec((tk, tm), lambda i, j, k: (k, i))
    else:
        a_spec = pl.BlockSpec((tm, tk), lambda i, j, k: (i, k))
    if b_cm and mode == "nn":
        assert tn == b.shape[2] and nk == 1
        b_spec = pl.BlockSpec((None, tk, tn), lambda i, j, k: (j, k, 0))
    elif b_cm:
        assert mode == "nt" and tk == b.shape[2]
        b_spec = pl.BlockSpec((None, tn, tk), lambda i, j, k: (k, j, 0))
    elif mode == "nt":
        b_spec = pl.BlockSpec((tn, tk), lambda i, j, k: (j, k))
    else:
        b_spec = pl.BlockSpec((tk, tn), lambda i, j, k: (k, j + b_col0))
    dims = {"nn": NN, "nt": NT, "tn": TN}[mode]
    o_spec = pl.BlockSpec((tm, tn), lambda i, j, k: (i, j))
    in_specs = [a_spec, b_spec]
    args = [a, b]
    if epi == "relu2_bwd":
        in_specs.append(o_spec)
        args.append(extra)
    if epi == "relu2":
        out_shape = (jax.ShapeDtypeStruct((M, N), F32), jax.ShapeDtypeStruct((M, N), BF16))
        out_specs = (o_spec, o_spec)
    elif out_cm:
        out_shape = jax.ShapeDtypeStruct((N // tn, M, tn), out_dtype)
        out_specs = pl.BlockSpec((None, tm, tn), lambda i, j, k: (j, i, 0))
    else:
        out_shape = jax.ShapeDtypeStruct((M, N), out_dtype)
        out_specs = o_spec
    scratch = [pltpu.VMEM((tm, tn), F32)] if nk > 1 else []
    n_main_in = len(args)
    n_main_out = 2 if epi == "relu2" else 1
    n_side = 0
    if side is not None:
        side_start, side_finish, side_arrays, side_out, n_sems = side
        n_side = len(side_arrays)
        in_specs += [ANY] * n_side
        args += list(side_arrays)
        out_shape = ((out_shape,) if n_main_out == 1 else tuple(out_shape)) + tuple(side_out)
        out_specs = ((out_specs,) if n_main_out == 1 else tuple(out_specs)) + tuple([ANY] * n_side)
        scratch += [pltpu.SemaphoreType.DMA((n_sems,)), pltpu.SemaphoreType.DMA((n_sems,))]
    grid = (M // tm, N // tn, nk)

    def body(*refs):
        a_ref, b_ref = refs[0], refs[1]
        rest = refs[2:n_main_in] + refs[n_main_in + n_side:n_main_in + n_side + n_main_out]
        acc_ref = refs[n_main_in + 2 * n_side + n_main_out] if nk > 1 else None
        i, j, k = pl.program_id(0), pl.program_id(1), pl.program_id(2)
        if n_side:
            side_refs = (refs[n_main_in:n_main_in + n_side],
                         refs[n_main_in + n_side + n_main_out:n_main_in + 2 * n_side + n_main_out], refs[-2], refs[-1])

            @pl.when((i == 0) & (j == 0) & (k == 0))
            def _():
                side_start(*side_refs)

        def finish(r):
            if epi == "relu2":
                rest[0][...] = r
                rest[1][...] = jnp.square(jnp.maximum(r, 0.0)).astype(BF16)
            elif epi == "relu2_bwd":
                rest[1][...] = (r * (2.0 * jnp.maximum(rest[0][...], 0.0))).astype(out_dtype)
            else:
                rest[0][...] = r.astype(out_dtype)

        p = _dot(a_ref[...], b_ref[...], dims, prec)
        if nk == 1:
            finish(p)
        else:
            @pl.when(k == 0)
            def _():
                acc_ref[...] = p

            @pl.when(k > 0)
            def _():
                acc_ref[...] += p

            @pl.when(k == nk - 1)
            def _():
                finish(acc_ref[...])

        if n_side:
            @pl.when((i == grid[0] - 1) & (j == grid[1] - 1) & (k == nk - 1))
            def _():
                side_finish(*side_refs)

    return pl.pallas_call(
        body, name=name, out_shape=out_shape, grid=grid,
        in_specs=in_specs, out_specs=out_specs, scratch_shapes=scratch,
        compiler_params=_cparams(*(("arbitrary",) * 3 if n_side else ("parallel", "parallel", "arbitrary"))),
    )(*args)


def _row_tile(L):
    return min(512, L)


def _rb(tm, w, cb=0):
    return pl.BlockSpec((tm, w), lambda i: (i, cb))


def _vb(r, w):
    return pl.BlockSpec((r, w), lambda i: (0, 0))


def _halo_prev(tm, w, cb=0):
    return pl.BlockSpec((8, w), lambda i: (jnp.maximum(i * (tm // 8) - 1, 0), cb))


def _halo_next(tm, w, nrow8, cb=0):
    return pl.BlockSpec((8, w), lambda i: (jnp.minimum((i + 1) * (tm // 8), nrow8 - 1), cb))


def _shift_down(x, halo, s):
    xr = pltpu.roll(x, s, 0)
    hr = pltpu.roll(halo, s, 0)
    row = lax.broadcasted_iota(jnp.int32, halo.shape, 0)
    top = jnp.where(row < s, hr, xr[0:8])
    return jnp.concatenate([top, xr[8:]], axis=0)


def _shift_up(x, halo, s):
    tm = x.shape[0]
    xr = pltpu.roll(x, tm - s, 0)
    hr = pltpu.roll(halo, 8 - s, 0)
    row = lax.broadcasted_iota(jnp.int32, halo.shape, 0)
    bot = jnp.where(row >= 8 - s, hr, xr[tm - 8:])
    return jnp.concatenate([xr[:tm - 8], bot], axis=0)


def _resid_norm(h, br, gate, ln, scale, shift, name):
    L = h.shape[0]
    tm = _row_tile(L)
    has_br = br is not None

    def body(*refs):
        if has_br:
            h_ref, br_ref, g_ref, ln_ref, sc_ref, sh_ref, hn_ref, u_ref = refs
            x = h_ref[...] + g_ref[...] * br_ref[...]
            hn_ref[...] = x
        else:
            h_ref, ln_ref, sc_ref, sh_ref, u_ref = refs
            x = h_ref[...]
        r = lax.rsqrt(jnp.mean(x * x, axis=-1, keepdims=True) + EPS)
        u = (x * r) * ln_ref[...] * (1.0 + sc_ref[...]) + sh_ref[...]
        u_ref[...] = u.astype(BF16)

    row, vec = _rb(tm, D), _vb(1, D)
    if has_br:
        return pl.pallas_call(
            body, name=name, grid=(L // tm,),
            out_shape=(jax.ShapeDtypeStruct((L, D), F32), jax.ShapeDtypeStruct((L, D), BF16)),
            in_specs=[row, row, vec, vec, vec, vec], out_specs=(row, row),
            compiler_params=_cparams("parallel"))(h, br, gate, ln, scale, shift)
    return pl.pallas_call(
        body, name=name, grid=(L // tm,), out_shape=jax.ShapeDtypeStruct((L, D), BF16),
        in_specs=[row, vec, vec, vec], out_specs=row,
        compiler_params=_cparams("parallel"))(h, ln, scale, shift)


def _conv_fwd(proj, conv_w8):
    L = proj.shape[0]
    tm = _row_tile(L)

    def body(cb_ref, cc_ref, cx_ref, cch_ref, cxh_ref, w_ref, o_ref):
        i = pl.program_id(0)
        q = cc_ref[...] * cx_ref[...]
        qh = jnp.where(i > 0, cch_ref[...] * cxh_ref[...], 0.0)
        cq = w_ref[2:3, :] * q + w_ref[1:2, :] * _shift_down(q, qh, 1) + w_ref[0:1, :] * _shift_down(q, qh, 2)
        o_ref[...] = (cb_ref[...] * cq).astype(BF16)

    return pl.pallas_call(
        body, name="conv_fwd", grid=(L // tm,), out_shape=jax.ShapeDtypeStruct((L, D), BF16),
        in_specs=[_rb(tm, D, 0), _rb(tm, D, 1), _rb(tm, D, 2), _halo_prev(tm, D, 1), _halo_prev(tm, D, 2), _vb(8, D)],
        out_specs=_rb(tm, D), compiler_params=_cparams("parallel"))(proj, proj, proj, proj, proj, conv_w8)


def _xbc_block(j):
    return jnp.where(j == 0, XBC_BLOCKS[0], XBC_BLOCKS[1] - 1 + j)


def _ssm_pre_fwd(proj, w8, bias):
    L = proj.shape[0]
    tm = _row_tile(L)

    def body(x_ref, xh_ref, w_ref, b_ref, xc_ref):
        i = pl.program_id(1)
        x = x_ref[...]
        xh = jnp.where(i > 0, xh_ref[...], 0.0)
        xc_ref[...] = (w_ref[3:4, :] * x + w_ref[2:3, :] * _shift_down(x, xh, 1) + w_ref[1:2, :] * _shift_down(x, xh, 2)
                       + w_ref[0:1, :] * _shift_down(x, xh, 3) + b_ref[...])

    out = jax.ShapeDtypeStruct((L, XBC), F32)
    o_spec = pl.BlockSpec((tm, CBLK), lambda j, i: (i, j))
    return pl.pallas_call(
        body, name="ssm_pre_fwd", grid=(4, L // tm), out_shape=out,
        in_specs=[pl.BlockSpec((tm, CBLK), lambda j, i: (i, _xbc_block(j))),
                  pl.BlockSpec((8, CBLK), lambda j, i: (jnp.maximum(i * (tm // 8) - 1, 0), _xbc_block(j))),
                  pl.BlockSpec((8, CBLK), lambda j, i: (0, j)),
                  pl.BlockSpec((1, CBLK), lambda j, i: (0, j))],
        out_specs=o_spec, compiler_params=_cparams("parallel", "parallel"))(proj, proj, w8, bias)


def _dot2(x, m01):
    hi = x.astype(BF16)
    lo = (x - hi.astype(F32)).astype(BF16)
    return _dot(hi, m01, NN) + _dot(lo, m01, NN)


def _expand4(v, h0, hid):
    return jnp.where(hid == 0, v[:, h0:h0 + 1],
                     jnp.where(hid == 1, v[:, h0 + 1:h0 + 2],
                               jnp.where(hid == 2, v[:, h0 + 2:h0 + 3], v[:, h0 + 3:h0 + 4])))


def _split2_rows(*vs):
    v = jnp.concatenate(vs, axis=0)
    hi = v.astype(BF16)
    return hi, (v - hi.astype(F32)).astype(BF16)


def _expand_heads(hi, lo, h0):
    kk = lax.broadcasted_iota(jnp.int32, (128, GW), 0)
    jj = lax.broadcasted_iota(jnp.int32, (128, GW), 1)
    pick = (kk == h0 + jj // HEAD_DIM).astype(BF16)
    return _dot(hi, pick, NN) + _dot(lo, pick, NN)


def _chunk_decay(dtr_ref, dtb_ref, alog_ref):
    dt_in = dtr_ref[...] + dtb_ref[...]
    dt = _softplus(dt_in)
    A = -jnp.exp(alog_ref[...])
    a = dt * A
    ti = lax.broadcasted_iota(jnp.int32, (Q, Q), 0)
    si = lax.broadcasted_iota(jnp.int32, (Q, Q), 1)
    tril = ti >= si
    acum = _dot_exact_lhs01(tril.astype(BF16), a)
    last = acum[Q - 1:Q, :]
    return dt_in, dt, A, acum, last, tril


def _ssd_fwd(xa, dtraw, dt_bias, a_log, dskip_x):
    L = xa.shape[0]
    nc = L // Q

    def body(xa_ref, dtr_ref, dtb_ref, alog_ref, dsk_ref, y_ref, st_ref, S):
        c = pl.program_id(0)

        @pl.when(c == 0)
        def _():
            S[...] = jnp.zeros_like(S)

        st_ref[0] = S[...]
        _, dt, _, acum, last, tril = _chunk_decay(dtr_ref, dtb_ref, alog_ref)
        acum_t = acum.T
        e = jnp.exp(acum)
        w = jnp.exp(last - acum)
        e_last = jnp.exp(last)
        hid = lax.broadcasted_iota(jnp.int32, (Q, GW), 1) // HEAD_DIM
        hid2 = lax.broadcasted_iota(jnp.int32, (Q, 128), 1) // HEAD_DIM
        rid = lax.broadcasted_iota(jnp.int32, (GW, STATE), 0) // HEAD_DIM
        for g in range(GROUPS):
            h0 = 4 * g
            bg = _silu(xa_ref[:, D_SSM + STATE * g:D_SSM + STATE * (g + 1)]).astype(BF16)
            cg = _silu(xa_ref[:, D_SSM + GROUPS * STATE + STATE * g:D_SSM + GROUPS * STATE + STATE * (g + 1)]).astype(BF16)
            cb = _dot(cg, bg, NT)
            sg = S[GW * g:GW * (g + 1), :]
            yoff = _dot(cg, sg.astype(BF16), NT)
            xg = _silu(xa_ref[:, GW * g:GW * (g + 1)])
            xdt = xg * _expand4(dt, h0, hid)
            xb = xdt.astype(BF16)
            yd = [jnp.zeros((Q, 128), F32), jnp.zeros((Q, 128), F32)]
            for r in range(4):
                h = h0 + r
                seg = acum[:, h:h + 1] - acum_t[h:h + 1, :]
                lm = jnp.where(tril, jnp.exp(jnp.minimum(seg, 0.0)), 0.0)
                m = (cb * lm).astype(BF16)
                xbp = xb[:, 128 * (r // 2):128 * (r // 2 + 1)]
                yd[r // 2] = yd[r // 2] + _dot(m, jnp.where(hid2 == r % 2, xbp, jnp.zeros_like(xbp)), NN)
            y_ref[:, GW * g:GW * (g + 1)] = (jnp.concatenate(yd, axis=1) + yoff * _expand4(e, h0, hid)
                                             + dsk_ref[:, GW * g:GW * (g + 1)] * xg)
            xw = (xdt * _expand4(w, h0, hid)).astype(BF16)
            upd = _dot(xw, bg, TN)
            el = jnp.where(rid == 0, e_last[:, h0:h0 + 1],
                           jnp.where(rid == 1, e_last[:, h0 + 1:h0 + 2],
                                     jnp.where(rid == 2, e_last[:, h0 + 2:h0 + 3], e_last[:, h0 + 3:h0 + 4])))
            S[GW * g:GW * (g + 1), :] = sg * el + upd

    return pl.pallas_call(
        body, name="ssd_fwd", grid=(nc,),
        out_shape=(jax.ShapeDtypeStruct((L, D_SSM), F32), jax.ShapeDtypeStruct((nc, D_SSM, STATE), F32)),
        in_specs=[_rb(Q, XBC), _rb(Q, 128), _vb(1, 128), _vb(1, 128), _vb(1, D_SSM)],
        out_specs=(_rb(Q, D_SSM), pl.BlockSpec((1, D_SSM, STATE), lambda i: (i, 0, 0))),
        scratch_shapes=[pltpu.VMEM((D_SSM, STATE), F32)],
        compiler_params=_cparams("arbitrary"))(xa, dtraw, dt_bias, a_log, dskip_x)


def _ssd_bwd(xa, dtraw, dt_bias, a_log, dskip_x, dy, states, dproj):
    L = xa.shape[0]
    nc = L // Q

    def body(xa_ref, dtr_ref, dtb_ref, alog_ref, dsk_ref, dy_ref, st_ref, dp_in,
             dxa_ref, ddt_ref, acc_ref, dska_ref, dS):
        del dp_in
        c = pl.program_id(0)

        @pl.when(c == 0)
        def _():
            dS[...] = jnp.zeros_like(dS)
            acc_ref[...] = jnp.zeros_like(acc_ref)
            dska_ref[...] = jnp.zeros_like(dska_ref)

        dt_in, dt, A, acum, last, tril = _chunk_decay(dtr_ref, dtb_ref, alog_ref)
        acum_t = acum.T
        e = jnp.exp(acum)
        w = jnp.exp(last - acum)
        e_last = jnp.exp(last)
        hid2 = lax.broadcasted_iota(jnp.int32, (Q, 128), 1) // HEAD_DIM
        triu = jnp.logical_not(tril) | (lax.broadcasted_iota(jnp.int32, (Q, Q), 0) == lax.broadcasted_iota(jnp.int32, (Q, Q), 1))
        per_head_hi, per_head_lo = _split2_rows(dt, e, w)
        rid = lax.broadcasted_iota(jnp.int32, (GW, STATE), 0) // HEAD_DIM
        rid1 = lax.broadcasted_iota(jnp.int32, (GW, 1), 0) // HEAD_DIM
        lane = lax.broadcasted_iota(jnp.int32, (Q, 128), 1)
        sub = lax.broadcasted_iota(jnp.int32, (128, Q), 0)
        lane1 = lax.broadcasted_iota(jnp.int32, (1, 128), 1)
        segr = lax.broadcasted_iota(jnp.int32, (GW, 128), 0) // HEAD_DIM
        segl = lax.broadcasted_iota(jnp.int32, (GW, 128), 1)
        dacum = jnp.zeros((Q, 128), F32)
        dacum_t = jnp.zeros((128, Q), F32)
        ddt = jnp.zeros((Q, 128), F32)
        dlast = jnp.zeros((1, 128), F32)
        for g in range(GROUPS):
            h0 = 4 * g
            bcol = D_SSM + STATE * g
            ccol = D_SSM + GROUPS * STATE + STATE * g
            bg = _silu(xa_ref[:, bcol:bcol + STATE]).astype(BF16)
            cg = _silu(xa_ref[:, ccol:ccol + STATE]).astype(BF16)
            sg = st_ref[0, GW * g:GW * (g + 1), :]
            sgb = sg.astype(BF16)
            dsg = dS[GW * g:GW * (g + 1), :]
            dsgb = dsg.astype(BF16)
            xg = _silu(xa_ref[:, GW * g:GW * (g + 1)])
            dyg = dy_ref[:, GW * g:GW * (g + 1)]
            spread = _expand_heads(per_head_hi, per_head_lo, h0)
            dtx, ex, wx = spread[0:Q], spread[Q:2 * Q], spread[2 * Q:3 * Q]
            el = jnp.where(rid == 0, e_last[:, h0:h0 + 1],
                           jnp.where(rid == 1, e_last[:, h0 + 1:h0 + 2],
                                     jnp.where(rid == 2, e_last[:, h0 + 2:h0 + 3], e_last[:, h0 + 3:h0 + 4])))
            eseg = (segr + h0 == segl).astype(BF16)
            xdt = xg * dtx
            xb = xdt.astype(BF16)
            xw = xdt * wx
            xwb = xw.astype(BF16)
            cb = _dot(cg, bg, NT)
            cb_t = _dot(bg, cg, NT)
            cs = _dot(cg, sgb, NT)
            dye = dyg * ex
            dyeb = dye.astype(BF16)
            dc = _dot(dyeb, sgb, NN)
            ds_prev = el * dsg + _dot(dyeb, cg, TN)
            bds = _dot(bg, dsgb, NT)
            dx = wx * bds
            dcb = jnp.zeros((Q, Q), F32)
            dxp = [jnp.zeros((Q, 128), F32), jnp.zeros((Q, 128), F32)]
            for r in range(4):
                h = h0 + r
                sl = slice(128 * (r // 2), 128 * (r // 2 + 1))
                seg = acum[:, h:h + 1] - acum_t[h:h + 1, :]
                lm = jnp.where(tril, jnp.exp(jnp.minimum(seg, 0.0)), 0.0)
                mf = cb * lm
                dyr = jnp.where(hid2 == r % 2, dyg[:, sl], 0.0).astype(BF16)
                dm = _dot(dyr, xb[:, sl], NT)
                dcb = dcb + dm * lm
                nh = dm * mf
                dacum = dacum + jnp.where(lane == h, jnp.sum(nh, axis=1, keepdims=True), 0.0)
                dacum_t = dacum_t + jnp.where(sub == h, jnp.sum(nh, axis=0, keepdims=True), 0.0)
                mt = cb_t * jnp.where(triu, jnp.exp(jnp.minimum(-seg, 0.0)), 0.0)
                dxp[r // 2] = dxp[r // 2] + _dot(mt.astype(BF16), dyr, NN)
            dx = dx + jnp.concatenate(dxp, axis=1)
            dcbb = dcb.astype(BF16)
            db = _dot(dcbb, cg, TN) + _dot(xwb, dsgb, NN)
            dc = dc + _dot(dcbb, bg, NN)
            t2 = xw * bds
            dacum = dacum + _dot2(dye * cs - t2, eseg)
            t2c = jnp.broadcast_to(jnp.sum(t2, axis=0, keepdims=True), (8, GW))
            dlast = dlast + _dot2(t2c, eseg)[0:1, :]
            v = jnp.sum(dsg * sg * el, axis=1, keepdims=True)
            for r in range(4):
                s_r = jnp.sum(jnp.where(rid1 == r, v, 0.0), axis=0, keepdims=True)
                dlast = dlast + jnp.where(lane1 == h0 + r, s_r, 0.0)
            ddt = ddt + _dot((dx * xg).astype(BF16), eseg, NN)
            dxa_ref[:, GW * g:GW * (g + 1)] = dx * dtx + dyg * dsk_ref[:, GW * g:GW * (g + 1)]
            dxa_ref[:, bcol:bcol + STATE] = db
            dxa_ref[:, ccol:ccol + STATE] = dc
            dska_ref[:, GW * g:GW * (g + 1)] += jnp.sum(dyg * xg, axis=0, keepdims=True)
            dS[GW * g:GW * (g + 1), :] = ds_prev
        dac = dacum - dacum_t.T
        rowq = lax.broadcasted_iota(jnp.int32, (Q, 128), 0)
        dac = dac + jnp.where(rowq == Q - 1, dlast, 0.0)
        ti = lax.broadcasted_iota(jnp.int32, (Q, Q), 0)
        si = lax.broadcasted_iota(jnp.int32, (Q, Q), 1)
        da = _dot_exact_lhs01((si >= ti).astype(BF16), dac)
        ddt_tot = ddt + da * A
        ddtraw = jnp.where(lane < HEADS, ddt_tot * _sigmoid(dt_in), 0.0)
        acc_ref[0:1, :] += jnp.sum(da * dt, axis=0, keepdims=True) * A
        acc_ref[1:2, :] += jnp.sum(ddtraw, axis=0, keepdims=True)
        ddt_ref[:, 0:128] = ddtraw.astype(BF16)
        ddt_ref[:, 128:CBLK] = jnp.zeros((Q, CBLK - 128), BF16)

    rev = lambda i: (nc - 1 - i, 0)
    return pl.pallas_call(
        body, name="ssd_bwd", grid=(nc,),
        out_shape=(jax.ShapeDtypeStruct((L, XBC), F32), jax.ShapeDtypeStruct((L, N_PAD), BF16),
                   jax.ShapeDtypeStruct((8, 128), F32), jax.ShapeDtypeStruct((1, D_SSM), F32)),
        in_specs=[pl.BlockSpec((Q, XBC), rev), pl.BlockSpec((Q, 128), rev), _vb(1, 128), _vb(1, 128), _vb(1, D_SSM),
                  pl.BlockSpec((Q, D_SSM), rev), pl.BlockSpec((1, D_SSM, STATE), lambda i: (nc - 1 - i, 0, 0)),
                  pl.BlockSpec(memory_space=pl.ANY)],
        out_specs=(pl.BlockSpec((Q, XBC), rev), pl.BlockSpec((Q, CBLK), lambda i: (nc - 1 - i, DT_BLOCK)),
                   _vb(8, 128), _vb(1, D_SSM)),
        input_output_aliases={7: 1},
        scratch_shapes=[pltpu.VMEM((D_SSM, STATE), F32)],
        compiler_params=_cparams("arbitrary"))(xa, dtraw, dt_bias, a_log, dskip_x, dy, states, dproj)


def _ssm_post_fwd(y, proj, nw):
    L = y.shape[0]
    tm = _row_tile(L)

    def body(y_ref, z_ref, nw_ref, o_ref):
        yz = y_ref[...] * _silu(z_ref[...])
        for k in range(CBLK // GW):
            s = yz[:, GW * k:GW * (k + 1)]
            rg = lax.rsqrt(jnp.mean(s * s, axis=-1, keepdims=True) + EPS)
            o_ref[:, GW * k:GW * (k + 1)] = (s * rg * nw_ref[:, GW * k:GW * (k + 1)]).astype(BF16)

    blk = pl.BlockSpec((tm, CBLK), lambda j, i: (i, j))
    return pl.pallas_call(
        body, name="ssm_post_fwd", grid=(2, L // tm), out_shape=jax.ShapeDtypeStruct((L, D_SSM), BF16),
        in_specs=[blk, pl.BlockSpec((tm, CBLK), lambda j, i: (i, Z_BLOCK0 + j)), pl.BlockSpec((1, CBLK), lambda j, i: (0, j))],
        out_specs=blk, compiler_params=_cparams("parallel", "parallel"))(y, proj, nw)


def _ssm_post_bwd(dyn, y, proj, nw, dproj):
    L = y.shape[0]
    tm = _row_tile(L)

    def body(dyn_ref, y_ref, z_ref, nw_ref, dp_in, dy_ref, dz_ref, acc_ref):
        del dp_in
        i = pl.program_id(1)

        @pl.when(i == 0)
        def _():
            acc_ref[...] = jnp.zeros_like(acc_ref)

        z = z_ref[...]
        yv = y_ref[...]
        sz = _silu(z)
        yz = yv * sz
        dyn_v = dyn_ref[...]
        for k in range(CBLK // GW):
            sl = slice(GW * k, GW * (k + 1))
            s = yz[:, sl]
            rg = lax.rsqrt(jnp.mean(s * s, axis=-1, keepdims=True) + EPS)
            yhat = s * rg
            dn = dyn_v[:, sl]
            acc_ref[0:1, sl] += jnp.sum(dn * yhat, axis=0, keepdims=True)
            dyhat = dn * nw_ref[:, sl]
            dyz = rg * (dyhat - yhat * jnp.mean(dyhat * yhat, axis=-1, keepdims=True))
            dy_ref[:, sl] = dyz * sz[:, sl]
            dz_ref[:, sl] = (dyz * yv[:, sl] * _dsilu(z[:, sl])).astype(BF16)

    blk = pl.BlockSpec((tm, CBLK), lambda j, i: (i, j))
    zblk = pl.BlockSpec((tm, CBLK), lambda j, i: (i, Z_BLOCK0 + j))
    return pl.pallas_call(
        body, name="ssm_post_bwd", grid=(2, L // tm),
        out_shape=(jax.ShapeDtypeStruct((L, D_SSM), F32), jax.ShapeDtypeStruct((L, N_PAD), BF16),
                   jax.ShapeDtypeStruct((8, D_SSM), F32)),
        in_specs=[blk, blk, zblk, pl.BlockSpec((1, CBLK), lambda j, i: (0, j)), pl.BlockSpec(memory_space=pl.ANY)],
        out_specs=(blk, zblk, pl.BlockSpec((8, CBLK), lambda j, i: (0, j))),
        input_output_aliases={4: 1},
        compiler_params=_cparams("arbitrary", "arbitrary"))(dyn, y, proj, nw, dproj)


def _merge_fwd(proj, p_conv, p_ssm):
    L = proj.shape[0]
    tm = _row_tile(L)

    def body(gl_ref, pc_ref, ps_ref, o_ref):
        o_ref[...] = (_sigmoid(gl_ref[:, 0:D]) * pc_ref[...] + _sigmoid(gl_ref[:, D:2 * D]) * ps_ref[...]).astype(BF16)

    return pl.pallas_call(
        body, name="merge_fwd", grid=(L // tm,), out_shape=jax.ShapeDtypeStruct((L, D), BF16),
        in_specs=[_rb(tm, 2 * D, GL_BLOCK2K), _rb(tm, D), _rb(tm, D)], out_specs=_rb(tm, D),
        compiler_params=_cparams("parallel"))(proj, p_conv, p_ssm)


def _merge_bwd(dmerged, proj, p_conv, p_ssm):
    L = proj.shape[0]
    tm = _row_tile(L)

    def body(dm_ref, gl_ref, pc_ref, ps_ref, dpc_ref, dps_ref, dgl_ref):
        dm = dm_ref[...]
        sc = _sigmoid(gl_ref[:, 0:D])
        ss = _sigmoid(gl_ref[:, D:2 * D])
        dpc_ref[...] = (dm * sc).astype(BF16)
        dps_ref[...] = (dm * ss).astype(BF16)
        dgl_ref[:, 0:D] = (dm * pc_ref[...] * sc * (1.0 - sc)).astype(BF16)
        dgl_ref[:, D:2 * D] = (dm * ps_ref[...] * ss * (1.0 - ss)).astype(BF16)

    bf = jax.ShapeDtypeStruct((L, D), BF16)
    return pl.pallas_call(
        body, name="merge_bwd", grid=(L // tm,),
        out_shape=(bf, bf, jax.ShapeDtypeStruct((L, N_PAD), BF16)),
        in_specs=[_rb(tm, D), _rb(tm, 2 * D, GL_BLOCK2K), _rb(tm, D), _rb(tm, D)],
        out_specs=(_rb(tm, D), _rb(tm, D), _rb(tm, 2 * D, GL_BLOCK2K)),
        compiler_params=_cparams("parallel"))(dmerged, proj, p_conv, p_ssm)


def _conv_bwd(dyc, proj, conv_w8, dproj):
    L = proj.shape[0]
    tm = _row_tile(L)
    n8 = L // 8
    nt = L // tm

    def body(dy_ref, dyn_ref, cb_ref, cbn_ref, cc_ref, cch_ref, cx_ref, cxh_ref, w_ref, dp_in, d_ref, acc_ref):
        del dp_in
        i = pl.program_id(0)

        @pl.when(i == 0)
        def _():
            acc_ref[...] = jnp.zeros_like(acc_ref)

        cc, cx, cb = cc_ref[...], cx_ref[...], cb_ref[...]
        q = cc * cx
        qh = jnp.where(i > 0, cch_ref[...] * cxh_ref[...], 0.0)
        q1 = _shift_down(q, qh, 1)
        q2 = _shift_down(q, qh, 2)
        cq = w_ref[2:3, :] * q + w_ref[1:2, :] * q1 + w_ref[0:1, :] * q2
        dy = dy_ref[...]
        dcq = dy * cb
        dcqn = jnp.where(i < nt - 1, dyn_ref[...] * cbn_ref[...], 0.0)
        dcq1, dcq2 = _shift_up(dcq, dcqn, 1), _shift_up(dcq, dcqn, 2)
        dq = w_ref[2:3, :] * dcq + w_ref[1:2, :] * dcq1 + w_ref[0:1, :] * dcq2
        d_ref[:, 0:D] = (dy * cq).astype(BF16)
        d_ref[:, D:2 * D] = (dq * cx).astype(BF16)
        d_ref[:, 2 * D:3 * D] = (dq * cc).astype(BF16)
        acc_ref[2:3, :] += jnp.sum(dcq * q, axis=0, keepdims=True)
        acc_ref[1:2, :] += jnp.sum(dcq1 * q, axis=0, keepdims=True)
        acc_ref[0:1, :] += jnp.sum(dcq2 * q, axis=0, keepdims=True)

    return pl.pallas_call(
        body, name="conv_bwd", grid=(nt,),
        out_shape=(jax.ShapeDtypeStruct((L, N_PAD), BF16), jax.ShapeDtypeStruct((8, D), F32)),
        in_specs=[_rb(tm, D), _halo_next(tm, D, n8), _rb(tm, D, 0), _halo_next(tm, D, n8, 0),
                  _rb(tm, D, 1), _halo_prev(tm, D, 1), _rb(tm, D, 2), _halo_prev(tm, D, 2), _vb(8, D),
                  pl.BlockSpec(memory_space=pl.ANY)],
        out_specs=(_rb(tm, 3 * D, 0), _vb(8, D)),
        input_output_aliases={9: 0},
        compiler_params=_cparams("arbitrary"))(dyc, dyc, proj, proj, proj, proj, proj, proj, conv_w8, dproj)


def _ssm_pre_bwd(dxa, xc, proj, w8, dproj):
    L = proj.shape[0]
    tm = _row_tile(L)
    n8 = L // 8
    nt = L // tm

    def body(dxa_ref, dxan_ref, xc_ref, xcn_ref, x_ref, w_ref, dp_in, d_ref, acc_ref):
        del dp_in
        i = pl.program_id(1)

        @pl.when(i == 0)
        def _():
            acc_ref[...] = jnp.zeros_like(acc_ref)

        dxc = dxa_ref[...] * _dsilu(xc_ref[...])
        dxcn = jnp.where(i < nt - 1, dxan_ref[...] * _dsilu(xcn_ref[...]), 0.0)
        d1, d2, d3 = _shift_up(dxc, dxcn, 1), _shift_up(dxc, dxcn, 2), _shift_up(dxc, dxcn, 3)
        d_ref[...] = (w_ref[3:4, :] * dxc + w_ref[2:3, :] * d1 + w_ref[1:2, :] * d2 + w_ref[0:1, :] * d3).astype(BF16)
        x = x_ref[...]
        acc_ref[3:4, :] += jnp.sum(dxc * x, axis=0, keepdims=True)
        acc_ref[2:3, :] += jnp.sum(d1 * x, axis=0, keepdims=True)
        acc_ref[1:2, :] += jnp.sum(d2 * x, axis=0, keepdims=True)
        acc_ref[0:1, :] += jnp.sum(d3 * x, axis=0, keepdims=True)
        acc_ref[4:5, :] += jnp.sum(dxc, axis=0, keepdims=True)

    blk = pl.BlockSpec((tm, CBLK), lambda j, i: (i, j))
    nxt = pl.BlockSpec((8, CBLK), lambda j, i: (jnp.minimum((i + 1) * (tm // 8), n8 - 1), j))
    pblk = pl.BlockSpec((tm, CBLK), lambda j, i: (i, _xbc_block(j)))
    return pl.pallas_call(
        body, name="ssm_pre_bwd", grid=(4, nt),
        out_shape=(jax.ShapeDtypeStruct((L, N_PAD), BF16), jax.ShapeDtypeStruct((8, XBC), F32)),
        in_specs=[blk, nxt, blk, nxt, pblk,
                  pl.BlockSpec((8, CBLK), lambda j, i: (0, j)), pl.BlockSpec(memory_space=pl.ANY)],
        out_specs=(pblk, pl.BlockSpec((8, CBLK), lambda j, i: (0, j))),
        input_output_aliases={6: 0},
        compiler_params=_cparams("arbitrary", "arbitrary"))(dxa, dxa, xc, xc, proj, w8, dproj)


def _loss_head(h1, dn, gate2, fnorm, target):
    L = h1.shape[0]
    tm = _row_tile(L)

    def body(h_ref, dn_ref, g_ref, fn_ref, t_ref, dh_ref, ddn_ref, acc_ref):
        i = pl.program_id(0)

        @pl.when(i == 0)
        def _():
            acc_ref[...] = jnp.zeros_like(acc_ref)

        dnv = dn_ref[...]
        g = g_ref[...]
        x = h_ref[...] + g * dnv
        r = lax.rsqrt(jnp.mean(x * x, axis=-1, keepdims=True) + EPS)
        xhat = x * r
        diff = xhat * fn_ref[...] - t_ref[...]
        dy = diff * (1.0 / D)
        dxhat = dy * fn_ref[...]
        dh = r * (dxhat - xhat * jnp.mean(dxhat * xhat, axis=-1, keepdims=True))
        dh_ref[...] = dh
        ddn_ref[...] = (dh * g).astype(BF16)
        acc_ref[0:1, :] += jnp.sum(dy * xhat, axis=0, keepdims=True)
        acc_ref[1:2, :] += jnp.sum(dh * dnv, axis=0, keepdims=True)
        acc_ref[2:3, :] += 0.5 * jnp.sum(jnp.mean(diff * diff, axis=-1, keepdims=True), axis=0, keepdims=True)

    row, vec = _rb(tm, D), _vb(1, D)
    return pl.pallas_call(
        body, name="loss_head", grid=(L // tm,),
        out_shape=(jax.ShapeDtypeStruct((L, D), F32), jax.ShapeDtypeStruct((L, D), BF16), jax.ShapeDtypeStruct((8, D), F32)),
        in_specs=[row, row, vec, vec, row], out_specs=(row, row, _vb(8, D)),
        compiler_params=_cparams("arbitrary"))(h1, dn, gate2, fnorm, target)


def _norm_bwd(du, h, dh_in, ln, scale, br, gate, name):
    L = h.shape[0]
    tm = _row_tile(L)
    has_br = br is not None

    def body(*refs):
        if has_br:
            du_ref, h_ref, dhi_ref, ln_ref, sc_ref, br_ref, g_ref, dh_ref, dbr_ref, acc_ref = refs
        else:
            du_ref, h_ref, dhi_ref, ln_ref, sc_ref, dh_ref, acc_ref = refs
        i = pl.program_id(0)

        @pl.when(i == 0)
        def _():
            acc_ref[...] = jnp.zeros_like(acc_ref)

        x = h_ref[...]
        duv = du_ref[...]
        r = lax.rsqrt(jnp.mean(x * x, axis=-1, keepdims=True) + EPS)
        xhat = x * r
        dn = duv * (1.0 + sc_ref[...])
        dxhat = dn * ln_ref[...]
        dh = dhi_ref[...] + r * (dxhat - xhat * jnp.mean(dxhat * xhat, axis=-1, keepdims=True))
        dh_ref[...] = dh
        acc_ref[0:1, :] += jnp.sum(duv, axis=0, keepdims=True)
        acc_ref[1:2, :] += jnp.sum(duv * xhat * ln_ref[...], axis=0, keepdims=True)
        acc_ref[2:3, :] += jnp.sum(dn * xhat, axis=0, keepdims=True)
        if has_br:
            dbr_ref[...] = (dh * g_ref[...]).astype(BF16)
            acc_ref[3:4, :] += jnp.sum(dh * br_ref[...], axis=0, keepdims=True)

    row, vec = _rb(tm, D), _vb(1, D)
    f32o, acc = jax.ShapeDtypeStruct((L, D), F32), jax.ShapeDtypeStruct((8, D), F32)
    if has_br:
        return pl.pallas_call(
            body, name=name, grid=(L // tm,), out_shape=(f32o, jax.ShapeDtypeStruct((L, D), BF16), acc),
            in_specs=[row, row, row, vec, vec, row, vec], out_specs=(row, row, _vb(8, D)),
            compiler_params=_cparams("arbitrary"))(du, h, dh_in, ln, scale, br, gate)
    return pl.pallas_call(
        body, name=name, grid=(L // tm,), out_shape=(f32o, acc),
        in_specs=[row, row, row, vec, vec], out_specs=(row, _vb(8, D)),
        compiler_params=_cparams("arbitrary"))(du, h, dh_in, ln, scale)


def _layer_fwd(h_prev, br_prev, gate_prev, mod, sp, W, next_shards=None):
    s = {}
    tag = ""
    if br_prev is None:
        s["h0"] = h_prev
        s["u"] = _resid_norm(h_prev, None, None, sp["ln1"], mod[1], mod[0], "norm1" + tag)
    else:
        s["h0"], s["u"] = _resid_norm(h_prev, br_prev, gate_prev, sp["ln1"], mod[1], mod[0], "norm1" + tag)
    if next_shards is None:
        s["proj"] = _mm(s["u"], W["w_full"], "nn", n_out=N_MAIN, name="mm_proj" + tag)
    else:
        s["proj"], *s["next_gathered"] = _mm(s["u"], W["w_full"], "nn", n_out=N_MAIN, name="mm_proj_wgather",
                                             side=_gather_side(next_shards))
    s["dtraw"] = _mm(s["u"], W["w_full"], "nn", tn=128, n_out=128, b_col0=N_MAIN // 128, name="mm_dt" + tag)
    s["y_conv"] = _conv_fwd(s["proj"], sp["conv_w8"])
    s["p_conv"] = _mm(s["y_conv"], W["w_conv_out"], "nn", name="mm_pconv" + tag)
    s["xc"] = _ssm_pre_fwd(s["proj"], sp["ssm_conv_w8"], sp["ssm_conv_b"])
    s["y"], s["states"] = _ssd_fwd(s["xc"], s["dtraw"], sp["dt_bias"], sp["a_log"], sp["dskip_x"])
    s["yn"] = _ssm_post_fwd(s["y"], s["proj"], sp["ssm_norm_w"])
    s["p_ssm"] = _mm(s["yn"], W["w_ssm_out"], "nn", name="mm_pssm" + tag)
    s["merged"] = _merge_fwd(s["proj"], s["p_conv"], s["p_ssm"])
    s["mix"] = _mm(s["merged"], W["w_o"], "nn", name="mm_mix" + tag)
    s["h1"], s["u2"] = _resid_norm(s["h0"], s["mix"], mod[2], sp["ln2"], mod[4], mod[3], "norm2" + tag)
    s["a_up"], s["hid"] = _mm(s["u2"], W["w_up"], "nn", epi="relu2", b_cm=True, name="mm_up" + tag)
    s["dn"] = _mm(s["hid"], W["w_down"], "nn", name="mm_down" + tag)
    return s


def _layer_bwd(s, dh2, ddn, mod, sp, W, br_below, gate_below, rs_parts=None):
    tag = ""
    g = {}
    da_up = _mm(ddn, W["w_down"], "nt", out_dtype=BF16, epi="relu2_bwd", extra=s["a_up"], name="mm_dhid" + tag)
    g["w_down"] = _mm(s["hid"], ddn, "tn", name="mm_gdown" + tag)
    du2 = _mm(da_up, W["w_up"], "nt", b_cm=True, name="mm_du2" + tag)
    g["w_up"] = _mm(s["u2"], da_up, "tn", out_cm=True, name="mm_gup" + tag)
    dh1, dmix, acc2 = _norm_bwd(du2, s["h1"], dh2, sp["ln2"], mod[4], s["mix"], mod[2], "norm2_bwd" + tag)
    dmerged = _mm(dmix, W["w_o"], "nt", name="mm_dmerged" + tag)
    g["w_o"] = _mm(s["merged"], dmix, "tn", name="mm_go" + tag)
    dpc, dps, dproj = _merge_bwd(dmerged, s["proj"], s["p_conv"], s["p_ssm"])
    dyc = _mm(dpc, W["w_conv_out"], "nt", name="mm_dyconv" + tag)
    g["w_conv_out"] = _mm(s["y_conv"], dpc, "tn", name="mm_gconvout" + tag)
    dyn = _mm(dps, W["w_ssm_out"], "nt", name="mm_dyn" + tag)
    g["w_ssm_out"] = _mm(s["yn"], dps, "tn", name="mm_gssmout" + tag)
    dproj, conv_acc = _conv_bwd(dyc, s["proj"], sp["conv_w8"], dproj)
    dy, dproj, post_acc = _ssm_post_bwd(dyn, s["y"], s["proj"], sp["ssm_norm_w"], dproj)
    dxa, dproj, ssd_acc, dsk_acc = _ssd_bwd(s["xc"], s["dtraw"], sp["dt_bias"], sp["a_log"], sp["dskip_x"], dy,
                                            s["states"], dproj)
    dproj, pre_acc = _ssm_pre_bwd(dxa, s["xc"], s["proj"], sp["ssm_conv_w8"], dproj)
    rs_recv = None
    if rs_parts is None:
        du = _mm(dproj, W["w_full"], "nt", name="mm_du" + tag)
    else:
        du, *rs_recv = _mm(dproj, W["w_full"], "nt", name="mm_du_gradrs", side=_xchip_side(rs_parts))
    g["w_full"] = _mm(s["u"], dproj, "tn", name="mm_gin" + tag)
    if br_below is None:
        dh0, acc1 = _norm_bwd(du, s["h0"], dh1, sp["ln1"], mod[1], None, None, "norm1_bwd" + tag)
        ddn_below = None
    else:
        dh0, ddn_below, acc1 = _norm_bwd(du, s["h0"], dh1, sp["ln1"], mod[1], br_below, gate_below, "norm1_bwd" + tag)
    accs = dict(acc1=acc1, acc2=acc2, conv=conv_acc, post=post_acc, ssd=ssd_acc, dsk=dsk_acc, pre=pre_acc)
    return g, dh0, ddn_below, accs, rs_recv


def _prep_layer_weights(w):
    w_in = jnp.transpose(w["w_in"], (1, 0, 2)).reshape(D, N_PROJ)
    cols = [w_in[:, CBLK * b:CBLK * (b + 1)] for b in ORIG_BLOCK_ORDER]
    cols += [w_in[:, N_MAIN:], jnp.zeros((D, N_PAD - N_PROJ), w_in.dtype)]
    return dict(w_full=jnp.concatenate(cols, axis=1), w_conv_out=w["w_conv_out"].reshape(D, D),
                w_ssm_out=w["w_ssm_out"].reshape(D_SSM, D), w_o=w["w_o"].reshape(D, D), w_up=w["w_up"],
                w_down=w["w_down"].reshape(4 * D, D))


def _prep_small(p):
    pad8 = lambda a: jnp.pad(a, ((0, 8 - a.shape[0]), (0, 0)))
    pad128 = lambda a: jnp.pad(a, (0, 128 - a.shape[0]))[None, :]
    return dict(ln1=p["ln1"][None, :], ln2=p["ln2"][None, :], conv_w8=pad8(p["conv_w"]),
                ssm_conv_w8=pad8(p["ssm_conv_w"]), ssm_conv_b=p["ssm_conv_b"][None, :],
                dt_bias=pad128(p["dt_bias"]), a_log=pad128(p["a_log"]),
                dskip_x=jnp.repeat(p["d_skip"], HEAD_DIM)[None, :], ssm_norm_w=p["ssm_norm_w"][None, :])


def _unpermute_w_in_grad(gfull):
    inv = [ORIG_BLOCK_ORDER.index(b) for b in range(len(ORIG_BLOCK_ORDER))]
    cols = [gfull[:, CBLK * i:CBLK * (i + 1)] for i in inv]
    cols.append(gfull[:, N_MAIN:N_MAIN + (N_PROJ - N_MAIN)])
    return jnp.transpose(jnp.concatenate(cols, axis=1).reshape(D, 4, N_PROJ // 4), (1, 0, 2))


def _collect_layer_grads(g, accs, p):
    del p
    out = dict(w_in=_unpermute_w_in_grad(g["w_full"]), w_conv_out=g["w_conv_out"].reshape(4, D // 4, D),
               w_ssm_out=g["w_ssm_out"].reshape(4, D_SSM // 4, D), w_o=g["w_o"].reshape(4, D // 4, D),
               w_up=g["w_up"], w_down=g["w_down"].reshape(4, D, D))
    out["ln1"] = accs["acc1"][2]
    out["ln2"] = accs["acc2"][2]
    out["conv_w"] = accs["conv"][0:3]
    out["ssm_conv_w"] = accs["pre"][0:4]
    out["ssm_conv_b"] = accs["pre"][4]
    out["a_log"] = accs["ssd"][0, :HEADS]
    out["dt_bias"] = accs["ssd"][1, :HEADS]
    out["d_skip"] = accs["dsk"].reshape(HEADS, HEAD_DIM).sum(axis=-1)
    out["ssm_norm_w"] = accs["post"][0]
    out["dmod"] = [accs["acc1"][0], accs["acc1"][1], accs["acc2"][3], accs["acc2"][0], accs["acc2"][1]]
    return out


ANY = pl.BlockSpec(memory_space=pl.ANY)


def _my_pos():
    return lax.axis_index("x"), lax.axis_index("y"), lax.axis_index("c")


def _all_gather_small(x_shard, name):
    m_per, n = x_shard.shape

    def body(x_ref, out_ref, send_sems, recv_sems, local_sem):
        x, y, c = _my_pos()
        me, sibling = (x, y, c), (x, y, 1 - c)
        chips = [(1 - x, y), (x, 1 - y), (1 - x, 1 - y)]

        def rows(px, py, pc):
            return out_ref.at[pl.ds((4 * px + 2 * py + pc) * m_per, m_per), :]

        def copy(k, block, to, src=None):
            return pltpu.make_async_remote_copy(
                src_ref=rows(*block) if src is None else src, dst_ref=rows(*block),
                send_sem=send_sems.at[k], recv_sem=recv_sems.at[k], device_id=to, device_id_type=MESH)

        mine = pltpu.make_async_copy(x_ref, rows(*me), local_sem)
        mine.start()
        first = [copy(0, me, sibling, src=x_ref)]
        first += [copy(1 + j, me, (*chip, c), src=x_ref) for j, chip in enumerate(chips)]
        for cp in first:
            cp.start()
        passed = [copy(4 + j, (*chip, c), sibling) for j, chip in enumerate(chips)]
        for j, chip in enumerate(chips):
            copy(1 + j, (*chip, c), me).wait_recv()
            passed[j].start()
        copy(0, sibling, me).wait_recv()
        for j, chip in enumerate(chips):
            copy(4 + j, (*chip, 1 - c), me).wait_recv()
        for cp in first + passed:
            cp.wait_send()
        mine.wait()

    return pl.pallas_call(
        body, name=name, out_shape=jax.ShapeDtypeStruct((8 * m_per, n), x_shard.dtype),
        in_specs=[pl.BlockSpec(memory_space=pltpu.VMEM)], out_specs=pl.BlockSpec(memory_space=pltpu.VMEM),
        scratch_shapes=[pltpu.SemaphoreType.DMA((7,)), pltpu.SemaphoreType.DMA((7,)), pltpu.SemaphoreType.DMA],
        compiler_params=pltpu.CompilerParams(vmem_limit_bytes=VMEM_LIMIT_BYTES),
    )(x_shard)


BIG = ("w_in", "w_conv_out", "w_ssm_out", "w_o", "w_up", "w_down")
NBIG = len(BIG)


def _half(ref, c, lead):
    half = ref.shape[lead] // 2
    idx = (slice(None),) * lead + (pl.ds(c * half, half),)
    return ref.at[idx]


def _gather_weights(shards, name):
    n = len(shards)
    start, finish, _, out_shape, n_sems = _gather_side(shards)

    def body(*refs):
        side_refs = (refs[:n], refs[n:2 * n], refs[2 * n], refs[2 * n + 1])
        start(*side_refs)
        finish(*side_refs)

    return pl.pallas_call(
        body, name=name, out_shape=out_shape, in_specs=[ANY] * n, out_specs=tuple([ANY] * n),
        scratch_shapes=[pltpu.SemaphoreType.DMA((n_sems,)), pltpu.SemaphoreType.DMA((n_sems,))],
    )(*shards)


def _gather_side(shards):
    n = len(shards)

    def copies(srcs, outs, send_sems, recv_sems):
        x, y, c = _my_pos()
        chips = [(1 - x, y), (x, 1 - y), (1 - x, 1 - y)]

        def ici(w, j, slot):
            px, py = chips[j]
            return pltpu.make_async_remote_copy(
                src_ref=_half(srcs[w], c, 0), dst_ref=_half(outs[w].at[slot], c, 0),
                send_sem=send_sems.at[6 * w + j], recv_sem=recv_sems.at[6 * w + j],
                device_id=(px, py, c), device_id_type=MESH)

        def d2d(w, j, core):
            px, py = chips[j]
            piece = _half(outs[w].at[2 * px + py], core, 0)
            return pltpu.make_async_remote_copy(
                src_ref=piece, dst_ref=piece, send_sem=send_sems.at[6 * w + 3 + j], recv_sem=recv_sems.at[6 * w + 3 + j],
                device_id=(x, y, 1 - c), device_id_type=MESH)

        return ici, d2d, 2 * x + y, [2 * px + py for px, py in chips], c

    def start(*side_refs):
        ici, _, chip, _, _ = copies(*side_refs)
        for w in range(n):
            for j in range(3):
                ici(w, j, chip).start()

    def finish(*side_refs):
        ici, d2d, chip, peer_chips, c = copies(*side_refs)
        for w in range(n):
            for j in range(3):
                ici(w, j, peer_chips[j]).wait_recv()
                d2d(w, j, c).start()
        for w in range(n):
            for j in range(3):
                d2d(w, j, 1 - c).wait_recv()
        for w in range(n):
            for j in range(3):
                ici(w, j, chip).wait_send()
                d2d(w, j, c).wait_send()

    out_shape = tuple(jax.ShapeDtypeStruct((4,) + tuple(s.shape), s.dtype) for s in shards)
    return start, finish, list(shards), out_shape, 6 * n


def _pair_exchange(grads, name):
    n = len(grads)

    def body(*refs):
        srcs, outs = refs[:n], refs[n:2 * n]
        send_sems, recv_sems = refs[2 * n], refs[2 * n + 1]
        x, y, c = _my_pos()
        cps = [pltpu.make_async_remote_copy(
            src_ref=_half(srcs[w], 1 - c, 1), dst_ref=outs[w], send_sem=send_sems.at[w], recv_sem=recv_sems.at[w],
            device_id=(x, y, 1 - c), device_id_type=MESH) for w in range(n)]
        for cp in cps:
            cp.start()
        for cp in cps:
            cp.wait()

    return pl.pallas_call(
        body, name=name,
        out_shape=tuple(jax.ShapeDtypeStruct((4, g.shape[1] // 2, g.shape[2]), g.dtype) for g in grads),
        in_specs=[ANY] * n, out_specs=tuple([ANY] * n),
        scratch_shapes=[pltpu.SemaphoreType.DMA((n,)), pltpu.SemaphoreType.DMA((n,))],
    )(*grads)


def _xchip_exchange(parts, name):
    n = len(parts)
    start, finish, _, out_shape, n_sems = _xchip_side(parts)

    def body(*refs):
        side_refs = (refs[:n], refs[n:2 * n], refs[2 * n], refs[2 * n + 1])
        start(*side_refs)
        finish(*side_refs)

    return pl.pallas_call(
        body, name=name, out_shape=out_shape, in_specs=[ANY] * n, out_specs=tuple([ANY] * n),
        scratch_shapes=[pltpu.SemaphoreType.DMA((n_sems,)), pltpu.SemaphoreType.DMA((n_sems,))],
    )(*parts)


def _xchip_side(parts):
    n = len(parts)

    def copies(srcs, outs, send_sems, recv_sems):
        x, y, c = _my_pos()
        chips = [(1 - x, y), (x, 1 - y), (1 - x, 1 - y)]

        def copy(w, j, slot):
            px, py = chips[j]
            return pltpu.make_async_remote_copy(
                src_ref=srcs[w].at[2 * px + py], dst_ref=outs[w].at[slot],
                send_sem=send_sems.at[3 * w + j], recv_sem=recv_sems.at[3 * w + j],
                device_id=(px, py, c), device_id_type=MESH)

        return copy, 2 * x + y, [2 * px + py for px, py in chips]

    def start(*side_refs):
        copy, chip, _ = copies(*side_refs)
        for w in range(n):
            for j in range(3):
                copy(w, j, chip).start()

    def finish(*side_refs):
        copy, chip, peer_chips = copies(*side_refs)
        for w in range(n):
            for j in range(3):
                copy(w, j, peer_chips[j]).wait_recv()
        for w in range(n):
            for j in range(3):
                copy(w, j, chip).wait_send()

    out_shape = tuple(jax.ShapeDtypeStruct(p.shape, p.dtype) for p in parts)
    return start, finish, list(parts), out_shape, 3 * n


def _sib_exchange(parts, name):
    n = len(parts)

    def body(*refs):
        srcs, outs = refs[:n], refs[n:2 * n]
        send_sems, recv_sems = refs[2 * n], refs[2 * n + 1]
        x, y, c = _my_pos()
        cps = [pltpu.make_async_remote_copy(
            src_ref=srcs[w], dst_ref=outs[w], send_sem=send_sems.at[w], recv_sem=recv_sems.at[w],
            device_id=(x, y, 1 - c), device_id_type=MESH) for w in range(n)]
        for cp in cps:
            cp.start()
        for cp in cps:
            cp.wait()

    return pl.pallas_call(
        body, name=name, out_shape=tuple(jax.ShapeDtypeStruct(p.shape, p.dtype) for p in parts),
        in_specs=[ANY] * n, out_specs=tuple([ANY] * n),
        scratch_shapes=[pltpu.SemaphoreType.DMA((n,)), pltpu.SemaphoreType.DMA((n,))],
    )(*parts)


def _sum_tile(R, C, nbuf):
    for cand in (512, 256, 128, 64, 32, 16):
        if R % cand == 0 and cand * C * 4 * nbuf <= 12 * 1024 * 1024:
            return cand
    return R


def _pair_sum(g, r, core, name):
    n, R, C = g.shape
    half = R // 2
    tr = _sum_tile(half, C, 3)
    nb = half // tr

    def body(c_ref, g_ref, r_ref, o_ref):
        del c_ref
        o_ref[...] = (g_ref[...] + r_ref[...]).astype(BF16)

    return pl.pallas_call(
        body, name=name, out_shape=jax.ShapeDtypeStruct((n, half, C), BF16),
        grid_spec=pltpu.PrefetchScalarGridSpec(
            num_scalar_prefetch=1, grid=(n, nb),
            in_specs=[pl.BlockSpec((None, tr, C), lambda s, i, c_ref: (s, c_ref[0] * nb + i, 0)),
                      pl.BlockSpec((None, tr, C), lambda s, i, c_ref: (s, i, 0))],
            out_specs=pl.BlockSpec((None, tr, C), lambda s, i, c_ref: (s, i, 0))),
        compiler_params=_cparams("parallel", "parallel"))(core, g, r)


def _sum_lead(x, name):
    n, R, C = x.shape
    tr = _sum_tile(R, C, n + 1)

    def body(x_ref, o_ref):
        acc = x_ref[0].astype(F32)
        for s in range(1, n):
            acc = acc + x_ref[s].astype(F32)
        o_ref[...] = acc

    return pl.pallas_call(
        body, name=name, out_shape=jax.ShapeDtypeStruct((R, C), F32), grid=(R // tr,),
        in_specs=[pl.BlockSpec((n, tr, C), lambda i: (0, i, 0))], out_specs=pl.BlockSpec((tr, C), lambda i: (i, 0)),
        compiler_params=_cparams("parallel"))(x)


HIGHEST = lax.Precision.HIGHEST


def _ada_fwd(c_all, w_ada, b_sh):
    nl, _, cols = w_ada.shape

    def body(c_ref, w_ref, b_ref, o_ref):
        o_ref[0] = _dot(_silu(c_ref[...]), w_ref[0], NN, HIGHEST) + b_ref[0]

    return pl.pallas_call(
        body, name="ada_fwd", grid=(nl,), out_shape=jax.ShapeDtypeStruct((nl, 8, cols), F32),
        in_specs=[_vb(8, D), pl.BlockSpec((1, D, cols), lambda l: (l, 0, 0)), pl.BlockSpec((1, 1, cols), lambda l: (l, 0, 0))],
        out_specs=pl.BlockSpec((1, 8, cols), lambda l: (l, 0, 0)), compiler_params=_cparams("parallel"))(c_all, w_ada, b_sh)


def _ada_bwd(c_pad, dmod_sh):
    nl, _, cols = dmod_sh.shape

    def body(c_ref, d_ref, o_ref):
        o_ref[0] = _dot(_silu(c_ref[...]), d_ref[0], TN, HIGHEST)

    return pl.pallas_call(
        body, name="ada_bwd", grid=(nl,), out_shape=jax.ShapeDtypeStruct((nl, D, cols), F32),
        in_specs=[_vb(128, D), pl.BlockSpec((1, 128, cols), lambda l: (l, 0, 0))],
        out_specs=pl.BlockSpec((1, D, cols), lambda l: (l, 0, 0)), compiler_params=_cparams("parallel"))(c_pad, dmod_sh)


def _adamw(w, g, m, v, name):
    R, C = w.shape
    tr = R
    for cand in (512, 256, 128, 64, 32, 16, 8):
        if R % cand == 0 and cand * C * 4 <= 2 * 1024 * 1024:
            tr = cand
            break

    def body(w_ref, g_ref, m_ref, v_ref, d_ref, m2_ref, v2_ref):
        gv = g_ref[...]
        m2 = ADAM_B1 * m_ref[...] + (1.0 - ADAM_B1) * gv
        v2 = ADAM_B2 * v_ref[...] + (1.0 - ADAM_B2) * jnp.square(gv)
        m_hat = m2 / (1.0 - ADAM_B1 ** ADAM_STEP)
        v_hat = v2 / (1.0 - ADAM_B2 ** ADAM_STEP)
        d_ref[...] = -ADAM_LR * (m_hat / (jnp.sqrt(v_hat) + ADAM_EPS) + ADAM_WD * w_ref[...])
        m2_ref[...] = m2
        v2_ref[...] = v2

    blk = pl.BlockSpec((tr, C), lambda i: (i, 0))
    o = jax.ShapeDtypeStruct((R, C), F32)
    return pl.pallas_call(
        body, name=name, grid=(R // tr,), out_shape=(o, o, o), in_specs=[blk] * 4, out_specs=(blk, blk, blk),
        compiler_params=_cparams("parallel"))(w, g, m, v)


def _fill_own_slot(got, own, chip):
    return {k: lax.dynamic_update_index_in_dim(g, o, chip, 0) for k, g, o in zip(BIG, got, own)}


def _rs_pair_stage(grads, core_arr):
    gs = [grads[k] for k in BIG]
    from_sib = _pair_exchange(gs, "gradrs_pair")
    return [_pair_sum(g, r, core_arr, "gradrs_pairsum_" + k) for k, g, r in zip(BIG, gs, from_sib)]


def _rs_finish(pair, recv, chip, core):
    mine = [_sum_lead(lax.dynamic_update_index_in_dim(r, lax.dynamic_index_in_dim(p, chip, 0, keepdims=False), chip, 0),
                      "gradrs_sum_" + k) for k, r, p in zip(BIG, recv, pair)]
    other = _sib_exchange(mine, "gradrs_sib")
    return {k: jnp.where(core == 0, jnp.concatenate([a, b], axis=0), jnp.concatenate([b, a], axis=0))
            for k, a, b in zip(BIG, mine, other)}


SMALL_LAYER = (("ln1", D), ("ln2", D), ("conv_w", 3 * D), ("ssm_conv_w", 4 * XBC), ("ssm_conv_b", XBC), ("dt_bias", HEADS),
               ("a_log", HEADS), ("d_skip", HEADS), ("ssm_norm_w", D_SSM), ("dmod", 6 * D))
SMALL_ROWS = 136


def _pack_small(layer_parts, fnorm_g, loss):
    vecs = []
    for lp in layer_parts:
        for k, n in SMALL_LAYER:
            vecs.append(lp[k].reshape(n))
    vecs += [fnorm_g.reshape(D), loss.reshape(1)]
    flat = jnp.concatenate(vecs)
    return jnp.pad(flat, (0, SMALL_ROWS * D - flat.shape[0])).reshape(SMALL_ROWS, D)


def _unpack_small(flat2d):
    flat = flat2d.reshape(-1)
    layers, r = [], 0
    for _ in range(DEPTH):
        lp = {}
        for k, n in SMALL_LAYER:
            lp[k] = flat[r:r + n]
            r += n
        layers.append(lp)
    return layers, flat[r:r + D], flat[r + D]


def kernel(x, c, w_ada, b_ada, ln1, ln2, w_in, conv_w, ssm_conv_w, ssm_conv_b, dt_bias, a_log, d_skip, ssm_norm_w, w_conv_out, w_ssm_out, w_o, w_up, w_down, final_norm, loss_target, m_w_ada, m_b_ada, m_ln1, m_ln2, m_w_in, m_conv_w, m_ssm_conv_w, m_ssm_conv_b, m_dt_bias, m_a_log, m_d_skip, m_ssm_norm_w, m_w_conv_out, m_w_ssm_out, m_w_o, m_w_up, m_w_down, m_final_norm, v_w_ada, v_b_ada, v_ln1, v_ln2, v_w_in, v_conv_w, v_ssm_conv_w, v_ssm_conv_b, v_dt_bias, v_a_log, v_d_skip, v_ssm_norm_w, v_w_conv_out, v_w_ssm_out, v_w_o, v_w_up, v_w_down, v_final_norm):
    xi, yi, ci = _my_pos()
    chip = 2 * xi + yi
    dev = 4 * xi + 2 * yi + ci
    core_arr = jnp.reshape(ci, (1,)).astype(jnp.int32)
    h_in, target = x[0], loss_target[0]
    ada_cols = w_ada.shape[2]

    c_all = _all_gather_small(jnp.pad(c, ((0, 7), (0, 0))), "gather_c").reshape(8, 8, D)[:, 0]
    b_sh = lax.dynamic_slice_in_dim(b_ada, chip * ada_cols, ada_cols, axis=1)[:, None, :]
    mod_sh = _ada_fwd(c_all, w_ada, b_sh)
    mod_all = _all_gather_small(mod_sh.reshape(-1, D), "gather_mod").reshape(8, DEPTH, 8, ada_cols)
    mod_mine = lax.dynamic_index_in_dim(mod_all, dev, axis=2, keepdims=False)
    mod = jnp.concatenate([mod_mine[2 * s] for s in range(4)], axis=-1).reshape(DEPTH, 6, D)

    shards = dict(w_in=w_in, w_conv_out=w_conv_out, w_ssm_out=w_ssm_out, w_o=w_o, w_up=w_up, w_down=w_down)
    small = dict(ln1=ln1, ln2=ln2, conv_w=None, ssm_conv_w=None, ssm_conv_b=ssm_conv_b, dt_bias=dt_bias, a_log=a_log,
                 d_skip=d_skip, ssm_norm_w=ssm_norm_w)
    taps = jnp.concatenate([conv_w.reshape(-1), ssm_conv_w.reshape(-1)])
    taps = jnp.pad(taps, (0, 24 * D - taps.shape[0])).reshape(24, D)
    taps_all = _all_gather_small(taps, "gather_taps").reshape(8, 24 * D)
    cw_cols, sw_cols = conv_w.shape[2], ssm_conv_w.shape[2]
    n_cw = DEPTH * 3 * cw_cols
    conv_w_full = jnp.concatenate([taps_all[2 * s, :n_cw].reshape(DEPTH, 3, cw_cols) for s in range(4)], axis=-1)
    ssm_conv_w_full = jnp.concatenate(
        [taps_all[2 * s, n_cw:n_cw + DEPTH * 4 * sw_cols].reshape(DEPTH, 4, sw_cols) for s in range(4)], axis=-1)

    saved, Ws, sps = [], [], []
    h_prev, br_prev, gate_prev = h_in, None, None
    own_bf16 = [[shards[k][l].astype(BF16) for k in BIG] for l in range(DEPTH)]
    gathered = _fill_own_slot(_gather_weights(own_bf16[0], "wgather"), own_bf16[0], chip)
    for l in range(DEPTH):
        W = _prep_layer_weights(gathered)
        p = {k: v[l] for k, v in small.items() if v is not None}
        p["conv_w"], p["ssm_conv_w"] = conv_w_full[l], ssm_conv_w_full[l]
        sp = _prep_small(p)
        mrows = [mod[l, i:i + 1] for i in range(6)]
        nxt = own_bf16[l + 1] if l + 1 < DEPTH else None
        s = _layer_fwd(h_prev, br_prev, gate_prev, mrows, sp, W, nxt)
        if nxt is not None:
            gathered = _fill_own_slot(s.pop("next_gathered"), nxt, chip)
        saved.append(s)
        Ws.append(W)
        sps.append((sp, mrows))
        h_prev, br_prev, gate_prev = s["h1"], s["dn"], mrows[5]

    dh, ddn, head_acc = _loss_head(saved[-1]["h1"], saved[-1]["dn"], sps[-1][1][5], final_norm[None, :], target)
    dgate2 = head_acc[1]
    layer_small = [None] * DEPTH
    big_grads = [None] * DEPTH
    pair_above = None
    for l in reversed(range(DEPTH)):
        sp, mrows = sps[l]
        below = (saved[l - 1]["dn"], sps[l - 1][1][5]) if l > 0 else (None, None)
        g, dh, ddn, accs, rs_recv = _layer_bwd(saved[l], dh, ddn, mrows, sp, Ws[l], below[0], below[1], pair_above)
        if pair_above is not None:
            big_grads[l + 1] = _rs_finish(pair_above, rs_recv, chip, ci)
        lg = _collect_layer_grads(g, accs, None)
        lg["dmod"] = jnp.stack(lg["dmod"] + [dgate2])
        if l > 0:
            dgate2 = accs["acc1"][3]
        layer_small[l] = lg
        pair_above = _rs_pair_stage(lg, core_arr)
    big_grads[0] = _rs_finish(pair_above, _xchip_exchange(pair_above, "gradrs_x"), chip, ci)
    grad_x = dh[None]

    packed = _pack_small(layer_small, head_acc[0], head_acc[2, 0])
    parts = _all_gather_small(packed, "gather_small").reshape(8, SMALL_ROWS, D)
    tot_layers, g_fnorm, loss = _unpack_small(_sum_lead(parts, "sum_small"))
    dmod_all = parts.reshape(8, -1)
    off, per_layer = 0, sum(n for _, n in SMALL_LAYER)
    dmod_off = per_layer - 6 * D
    dmod_dev = jnp.stack([dmod_all[:, l * per_layer + dmod_off:(l + 1) * per_layer] for l in range(DEPTH)], axis=0)
    dmod_cols = lax.dynamic_slice_in_dim(dmod_dev, chip * ada_cols, ada_cols, axis=2)
    g_w_ada = _ada_bwd(jnp.pad(c_all, ((0, 120), (0, 0))), jnp.pad(dmod_cols, ((0, 0), (0, 120), (0, 0))))

    stack = lambda k: jnp.stack([tot_layers[l][k] for l in range(DEPTH)])
    grads = dict(
        w_ada=g_w_ada, b_ada=stack("dmod"), ln1=stack("ln1"), ln2=stack("ln2"),
        w_in=jnp.stack([big_grads[l]["w_in"] for l in range(DEPTH)]),
        conv_w=lax.dynamic_slice_in_dim(stack("conv_w").reshape(DEPTH, 3, D), chip * cw_cols, cw_cols, axis=2),
        ssm_conv_w=lax.dynamic_slice_in_dim(stack("ssm_conv_w").reshape(DEPTH, 4, XBC), chip * sw_cols, sw_cols, axis=2),
        ssm_conv_b=stack("ssm_conv_b"), dt_bias=stack("dt_bias"), a_log=stack("a_log"), d_skip=stack("d_skip"),
        ssm_norm_w=stack("ssm_norm_w"),
        w_conv_out=jnp.stack([big_grads[l]["w_conv_out"] for l in range(DEPTH)]),
        w_ssm_out=jnp.stack([big_grads[l]["w_ssm_out"] for l in range(DEPTH)]),
        w_o=jnp.stack([big_grads[l]["w_o"] for l in range(DEPTH)]),
        w_up=jnp.stack([big_grads[l]["w_up"] for l in range(DEPTH)]),
        w_down=jnp.stack([big_grads[l]["w_down"] for l in range(DEPTH)]),
        final_norm=g_fnorm)
    params = dict(w_ada=w_ada, b_ada=b_ada, ln1=ln1, ln2=ln2, w_in=w_in, conv_w=conv_w, ssm_conv_w=ssm_conv_w,
                  ssm_conv_b=ssm_conv_b, dt_bias=dt_bias, a_log=a_log, d_skip=d_skip, ssm_norm_w=ssm_norm_w,
                  w_conv_out=w_conv_out, w_ssm_out=w_ssm_out, w_o=w_o, w_up=w_up, w_down=w_down, final_norm=final_norm)
    ms = dict(w_ada=m_w_ada, b_ada=m_b_ada, ln1=m_ln1, ln2=m_ln2, w_in=m_w_in, conv_w=m_conv_w, ssm_conv_w=m_ssm_conv_w,
              ssm_conv_b=m_ssm_conv_b, dt_bias=m_dt_bias, a_log=m_a_log, d_skip=m_d_skip, ssm_norm_w=m_ssm_norm_w,
              w_conv_out=m_w_conv_out, w_ssm_out=m_w_ssm_out, w_o=m_w_o, w_up=m_w_up, w_down=m_w_down,
              final_norm=m_final_norm)
    vs = dict(w_ada=v_w_ada, b_ada=v_b_ada, ln1=v_ln1, ln2=v_ln2, w_in=v_w_in, conv_w=v_conv_w, ssm_conv_w=v_ssm_conv_w,
              ssm_conv_b=v_ssm_conv_b, dt_bias=v_dt_bias, a_log=v_a_log, d_skip=v_d_skip, ssm_norm_w=v_ssm_norm_w,
              w_conv_out=v_w_conv_out, w_ssm_out=v_w_ssm_out, w_o=v_w_o, w_up=v_w_up, w_down=v_w_down,
              final_norm=v_final_norm)
    names = list(params)
    deltas, new_m, new_v = {}, {}, {}
    for k in names:
        shp = params[k].shape
        two_d = (-1, shp[-1]) if len(shp) > 1 else (1, shp[0])
        gk = grads[k].reshape(shp)
        grads[k] = gk
        d_, m_, v_ = _adamw(params[k].reshape(two_d), gk.reshape(two_d), ms[k].reshape(two_d), vs[k].reshape(two_d),
                            "adamw_" + k)
        deltas[k], new_m[k], new_v[k] = d_.reshape(shp), m_.reshape(shp), v_.reshape(shp)
    return (loss, grad_x, *[grads[k] for k in names], *[deltas[k] for k in names], *[new_m[k] for k in names],
            *[new_v[k] for k in names])
```

```python
import jax
import jax.numpy as jnp
from jax import lax
from jax.experimental import pallas as pl
from jax.experimental.pallas import tpu as pltpu

F32, BF16 = jnp.float32, jnp.bfloat16
EPS = 1e-6
D = 1024
D_SSM = 2048
HEADS = 32
HEAD_DIM = 64
GROUPS = 8
STATE = 128
GW = D_SSM // GROUPS
XBC = 4096
N_PROJ = 11296
N_MAIN = 11264
N_PAD = 12288
DEPTH = 4
Q = 128
CBLK = 1024
ORIG_BLOCK_ORDER = (2, 3, 4, 7, 0, 1, 5, 6, 8, 9, 10)
XBC_BLOCKS = (3, 8, 9, 10)
Z_BLOCK0 = 6
GL_BLOCK2K = 2
DT_BLOCK = 11
VMEM_LIMIT_BYTES = 56 * 1024 * 1024
ADAM_LR, ADAM_B1, ADAM_B2, ADAM_EPS, ADAM_WD, ADAM_STEP = 0.001, 0.9, 0.999, 1e-08, 0.01, 10
MESH = pl.DeviceIdType.MESH


def _cparams(*sem):
    return pltpu.CompilerParams(dimension_semantics=sem, vmem_limit_bytes=VMEM_LIMIT_BYTES)


def _sigmoid(x):
    return 1.0 / (1.0 + jnp.exp(-x))


def _silu(x):
    return x * _sigmoid(x)


def _dsilu(x):
    s = _sigmoid(x)
    return s * (1.0 + x * (1.0 - s))


def _softplus(x):
    return jnp.maximum(x, 0.0) + jnp.log(1.0 + jnp.exp(-jnp.abs(x)))


def _dot(a, b, dims, prec=None):
    return lax.dot_general(a, b, (dims, ((), ())), preferred_element_type=F32, precision=prec)


NN = ((1,), (0,))
NT = ((1,), (1,))
TN = ((0,), (0,))


def _split3(x):
    hi = x.astype(BF16)
    r1 = x - hi.astype(F32)
    mid = r1.astype(BF16)
    lo = (r1 - mid.astype(F32)).astype(BF16)
    return hi, mid, lo


def _dot_exact_lhs01(m01, x):
    hi, mid, lo = _split3(x)
    return _dot(m01, hi, NN) + _dot(m01, mid, NN) + _dot(m01, lo, NN)


def _mm(a, b, mode, *, out_dtype=F32, tm=1024, tn=1024, tk=1024, name, epi=None, extra=None, prec=None,
        b_cm=False, out_cm=False, n_out=None, b_col0=0, side=None):
    if mode == "nn":
        (M, K), (K2, N) = a.shape, ((b.shape[1], b.shape[0] * b.shape[2]) if b_cm else b.shape)
    elif mode == "nt":
        (M, K), (N, K2) = a.shape, ((b.shape[1], b.shape[0] * b.shape[2]) if b_cm else b.shape)
    else:
        (K, M), (K2, N) = a.shape, b.shape
    assert K == K2, (a.shape, b.shape, mode)
    if n_out is not None:
        N = n_out
    tm, tn, tk = min(tm, M), min(tn, N), min(tk, K)
    assert M % tm == 0 and N % tn == 0 and K % tk == 0, (M, N, K, tm, tn, tk)
    nk = K // tk
    if mode == "tn":
        a_spec = pl.BlockSpec((tk, tm), lambda i, j, k: (k, i))
    else:
        a_spec = pl.BlockSpec((tm, tk), lambda i, j, k: (i, k))
    if b_cm and mode == "nn":
        assert tn == b.shape[2] and nk == 1
        b_spec = pl.BlockSpec((None, tk, tn), lambda i, j, k: (j, k, 0))
    elif b_cm:
        assert mode == "nt" and tk == b.shape[2]
        b_spec = pl.BlockSpec((None, tn, tk), lambda i, j, k: (k, j, 0))
    elif mode == "nt":
        b_spec = pl.BlockSpec((tn, tk), lambda i, j, k: (j, k))
    else:
        b_spec = pl.BlockSpec((tk, tn), lambda i, j, k: (k, j + b_col0))
    dims = {"nn": NN, "nt": NT, "tn": TN}[mode]
    o_spec = pl.BlockSpec((tm, tn), lambda i, j, k: (i, j))
    in_specs = [a_spec, b_spec]
    args = [a, b]
    if epi == "relu2_bwd":
        in_specs.append(o_spec)
        args.append(extra)
    if epi == "relu2":
        out_shape = (jax.ShapeDtypeStruct((M, N), F32), jax.ShapeDtypeStruct((M, N), BF16))
        out_specs = (o_spec, o_spec)
    elif out_cm:
        out_shape = jax.ShapeDtypeStruct((N // tn, M, tn), out_dtype)
        out_specs = pl.BlockSpec((None, tm, tn), lambda i, j, k: (j, i, 0))
    else:
        out_shape = jax.ShapeDtypeStruct((M, N), out_dtype)
        out_specs = o_spec
    scratch = [pltpu.VMEM((tm, tn), F32)] if nk > 1 else []
    n_main_in = len(args)
    n_main_out = 2 if epi == "relu2" else 1
    n_side = 0
    if side is not None:
        side_start, side_finish, side_arrays, side_out, n_sems = side
        n_side = len(side_arrays)
        in_specs += [ANY] * n_side
        args += list(side_arrays)
        out_shape = ((out_shape,) if n_main_out == 1 else tuple(out_shape)) + tuple(side_out)
        out_specs = ((out_specs,) if n_main_out == 1 else tuple(out_specs)) + tuple([ANY] * n_side)
        scratch += [pltpu.SemaphoreType.DMA((n_sems,)), pltpu.SemaphoreType.DMA((n_sems,))]
    grid = (M // tm, N // tn, nk)

    def body(*refs):
        a_ref, b_ref = refs[0], refs[1]
        rest = refs[2:n_main_in] + refs[n_main_in + n_side:n_main_in + n_side + n_main_out]
        acc_ref = refs[n_main_in + 2 * n_side + n_main_out] if nk > 1 else None
        i, j, k = pl.program_id(0), pl.program_id(1), pl.program_id(2)
        if n_side:
            side_refs = (refs[n_main_in:n_main_in + n_side],
                         refs[n_main_in + n_side + n_main_out:n_main_in + 2 * n_side + n_main_out], refs[-2], refs[-1])

            @pl.when((i == 0) & (j == 0) & (k == 0))
            def _():
                side_start(*side_refs)

        def finish(r):
            if epi == "relu2":
                rest[0][...] = r
                rest[1][...] = jnp.square(jnp.maximum(r, 0.0)).astype(BF16)
            elif epi == "relu2_bwd":
                rest[1][...] = (r * (2.0 * jnp.maximum(rest[0][...], 0.0))).astype(out_dtype)
            else:
                rest[0][...] = r.astype(out_dtype)

        p = _dot(a_ref[...], b_ref[...], dims, prec)
        if nk == 1:
            finish(p)
        else:
            @pl.when(k == 0)
            def _():
                acc_ref[...] = p

            @pl.when(k > 0)
            def _():
                acc_ref[...] += p

            @pl.when(k == nk - 1)
            def _():
                finish(acc_ref[...])

        if n_side:
            @pl.when((i == grid[0] - 1) & (j == grid[1] - 1) & (k == nk - 1))
            def _():
                side_finish(*side_refs)

    return pl.pallas_call(
        body, name=name, out_shape=out_shape, grid=grid,
        in_specs=in_specs, out_specs=out_specs, scratch_shapes=scratch,
        compiler_params=_cparams(*(("arbitrary",) * 3 if n_side else ("parallel", "parallel", "arbitrary"))),
    )(*args)


def _row_tile(L):
    return min(512, L)


def _rb(tm, w, cb=0):
    return pl.BlockSpec((tm, w), lambda i: (i, cb))


def _vb(r, w):
    return pl.BlockSpec((r, w), lambda i: (0, 0))


def _halo_prev(tm, w, cb=0):
    return pl.BlockSpec((8, w), lambda i: (jnp.maximum(i * (tm // 8) - 1, 0), cb))


def _halo_next(tm, w, nrow8, cb=0):
    return pl.BlockSpec((8, w), lambda i: (jnp.minimum((i + 1) * (tm // 8), nrow8 - 1), cb))


def _shift_down(x, halo, s):
    xr = pltpu.roll(x, s, 0)
    hr = pltpu.roll(halo, s, 0)
    row = lax.broadcasted_iota(jnp.int32, halo.shape, 0)
    top = jnp.where(row < s, hr, xr[0:8])
    return jnp.concatenate([top, xr[8:]], axis=0)


def _shift_up(x, halo, s):
    tm = x.shape[0]
    xr = pltpu.roll(x, tm - s, 0)
    hr = pltpu.roll(halo, 8 - s, 0)
    row = lax.broadcasted_iota(jnp.int32, halo.shape, 0)
    bot = jnp.where(row >= 8 - s, hr, xr[tm - 8:])
    return jnp.concatenate([xr[:tm - 8], bot], axis=0)


def _resid_norm(h, br, gate, ln, scale, shift, name):
    L = h.shape[0]
    tm = _row_tile(L)
    has_br = br is not None

    def body(*refs):
        if has_br:
            h_ref, br_ref, g_ref, ln_ref, sc_ref, sh_ref, hn_ref, u_ref = refs
            x = h_ref[...] + g_ref[...] * br_ref[...]
            hn_ref[...] = x
        else:
            h_ref, ln_ref, sc_ref, sh_ref, u_ref = refs
            x = h_ref[...]
        r = lax.rsqrt(jnp.mean(x * x, axis=-1, keepdims=True) + EPS)
        u = (x * r) * ln_ref[...] * (1.0 + sc_ref[...]) + sh_ref[...]
        u_ref[...] = u.astype(BF16)

    row, vec = _rb(tm, D), _vb(1, D)
    if has_br:
        return pl.pallas_call(
            body, name=name, grid=(L // tm,),
            out_shape=(jax.ShapeDtypeStruct((L, D), F32), jax.ShapeDtypeStruct((L, D), BF16)),
            in_specs=[row, row, vec, vec, vec, vec], out_specs=(row, row),
            compiler_params=_cparams("parallel"))(h, br, gate, ln, scale, shift)
    return pl.pallas_call(
        body, name=name, grid=(L // tm,), out_shape=jax.ShapeDtypeStruct((L, D), BF16),
        in_specs=[row, vec, vec, vec], out_specs=row,
        compiler_params=_cparams("parallel"))(h, ln, scale, shift)


def _conv_fwd(proj, conv_w8):
    L = proj.shape[0]
    tm = _row_tile(L)

    def body(cb_ref, cc_ref, cx_ref, cch_ref, cxh_ref, w_ref, o_ref):
        i = pl.program_id(0)
        q = cc_ref[...] * cx_ref[...]
        qh = jnp.where(i > 0, cch_ref[...] * cxh_ref[...], 0.0)
        cq = w_ref[2:3, :] * q + w_ref[1:2, :] * _shift_down(q, qh, 1) + w_ref[0:1, :] * _shift_down(q, qh, 2)
        o_ref[...] = (cb_ref[...] * cq).astype(BF16)

    return pl.pallas_call(
        body, name="conv_fwd", grid=(L // tm,), out_shape=jax.ShapeDtypeStruct((L, D), BF16),
        in_specs=[_rb(tm, D, 0), _rb(tm, D, 1), _rb(tm, D, 2), _halo_prev(tm, D, 1), _halo_prev(tm, D, 2), _vb(8, D)],
        out_specs=_rb(tm, D), compiler_params=_cparams("parallel"))(proj, proj, proj, proj, proj, conv_w8)


def _xbc_block(j):
    return jnp.where(j == 0, XBC_BLOCKS[0], XBC_BLOCKS[1] - 1 + j)


def _ssm_pre_fwd(proj, w8, bias):
    L = proj.shape[0]
    tm = _row_tile(L)

    def body(x_ref, xh_ref, w_ref, b_ref, xc_ref):
        i = pl.program_id(1)
        x = x_ref[...]
        xh = jnp.where(i > 0, xh_ref[...], 0.0)
        xc_ref[...] = (w_ref[3:4, :] * x + w_ref[2:3, :] * _shift_down(x, xh, 1) + w_ref[1:2, :] * _shift_down(x, xh, 2)
                       + w_ref[0:1, :] * _shift_down(x, xh, 3) + b_ref[...])

    out = jax.ShapeDtypeStruct((L, XBC), F32)
    o_spec = pl.BlockSpec((tm, CBLK), lambda j, i: (i, j))
    return pl.pallas_call(
        body, name="ssm_pre_fwd", grid=(4, L // tm), out_shape=out,
        in_specs=[pl.BlockSpec((tm, CBLK), lambda j, i: (i, _xbc_block(j))),
                  pl.BlockSpec((8, CBLK), lambda j, i: (jnp.maximum(i * (tm // 8) - 1, 0), _xbc_block(j))),
                  pl.BlockSpec((8, CBLK), lambda j, i: (0, j)),
                  pl.BlockSpec((1, CBLK), lambda j, i: (0, j))],
        out_specs=o_spec, compiler_params=_cparams("parallel", "parallel"))(proj, proj, w8, bias)


def _dot2(x, m01):
    hi = x.astype(BF16)
    lo = (x - hi.astype(F32)).astype(BF16)
    return _dot(hi, m01, NN) + _dot(lo, m01, NN)


def _expand4(v, h0, hid):
    return jnp.where(hid == 0, v[:, h0:h0 + 1],
                     jnp.where(hid == 1, v[:, h0 + 1:h0 + 2],
                               jnp.where(hid == 2, v[:, h0 + 2:h0 + 3], v[:, h0 + 3:h0 + 4])))


def _split2_rows(*vs):
    v = jnp.concatenate(vs, axis=0)
    hi = v.astype(BF16)
    return hi, (v - hi.astype(F32)).astype(BF16)


def _expand_heads(hi, lo, h0):
    kk = lax.broadcasted_iota(jnp.int32, (128, GW), 0)
    jj = lax.broadcasted_iota(jnp.int32, (128, GW), 1)
    pick = (kk == h0 + jj // HEAD_DIM).astype(BF16)
    return _dot(hi, pick, NN) + _dot(lo, pick, NN)


def _chunk_decay(dtr_ref, dtb_ref, alog_ref):
    dt_in = dtr_ref[...] + dtb_ref[...]
    dt = _softplus(dt_in)
    A = -jnp.exp(alog_ref[...])
    a = dt * A
    ti = lax.broadcasted_iota(jnp.int32, (Q, Q), 0)
    si = lax.broadcasted_iota(jnp.int32, (Q, Q), 1)
    tril = ti >= si
    acum = _dot_exact_lhs01(tril.astype(BF16), a)
    last = acum[Q - 1:Q, :]
    return dt_in, dt, A, acum, last, tril


def _ssd_fwd(xa, dtraw, dt_bias, a_log, dskip_x):
    L = xa.shape[0]
    nc = L // Q

    def body(xa_ref, dtr_ref, dtb_ref, alog_ref, dsk_ref, y_ref, st_ref, S):
        c = pl.program_id(0)

        @pl.when(c == 0)
        def _():
            S[...] = jnp.zeros_like(S)

        st_ref[0] = S[...]
        _, dt, _, acum, last, tril = _chunk_decay(dtr_ref, dtb_ref, alog_ref)
        acum_t = acum.T
        e = jnp.exp(acum)
        w = jnp.exp(last - acum)
        e_last = jnp.exp(last)
        hid = lax.broadcasted_iota(jnp.int32, (Q, GW), 1) // HEAD_DIM
        hid2 = lax.broadcasted_iota(jnp.int32, (Q, 128), 1) // HEAD_DIM
        rid = lax.broadcasted_iota(jnp.int32, (GW, STATE), 0) // HEAD_DIM
        for g in range(GROUPS):
            h0 = 4 * g
            bg = _silu(xa_ref[:, D_SSM + STATE * g:D_SSM + STATE * (g + 1)]).astype(BF16)
            cg = _silu(xa_ref[:, D_SSM + GROUPS * STATE + STATE * g:D_SSM + GROUPS * STATE + STATE * (g + 1)]).astype(BF16)
            cb = _dot(cg, bg, NT)
            sg = S[GW * g:GW * (g + 1), :]
            yoff = _dot(cg, sg.astype(BF16), NT)
            xg = _silu(xa_ref[:, GW * g:GW * (g + 1)])
            xdt = xg * _expand4(dt, h0, hid)
            xb = xdt.astype(BF16)
            yd = [jnp.zeros((Q, 128), F32), jnp.zeros((Q, 128), F32)]
            for r in range(4):
                h = h0 + r
                seg = acum[:, h:h + 1] - acum_t[h:h + 1, :]
                lm = jnp.where(tril, jnp.exp(jnp.minimum(seg, 0.0)), 0.0)
                m = (cb * lm).astype(BF16)
                xbp = xb[:, 128 * (r // 2):128 * (r // 2 + 1)]
                yd[r // 2] = yd[r // 2] + _dot(m, jnp.where(hid2 == r % 2, xbp, jnp.zeros_like(xbp)), NN)
            y_ref[:, GW * g:GW * (g + 1)] = (jnp.concatenate(yd, axis=1) + yoff * _expand4(e, h0, hid)
                                             + dsk_ref[:, GW * g:GW * (g + 1)] * xg)
            xw = (xdt * _expand4(w, h0, hid)).astype(BF16)
            upd = _dot(xw, bg, TN)
            el = jnp.where(rid == 0, e_last[:, h0:h0 + 1],
                           jnp.where(rid == 1, e_last[:, h0 + 1:h0 + 2],
                                     jnp.where(rid == 2, e_last[:, h0 + 2:h0 + 3], e_last[:, h0 + 3:h0 + 4])))
            S[GW * g:GW * (g + 1), :] = sg * el + upd

    return pl.pallas_call(
        body, name="ssd_fwd", grid=(nc,),
        out_shape=(jax.ShapeDtypeStruct((L, D_SSM), F32), jax.ShapeDtypeStruct((nc, D_SSM, STATE), F32)),
        in_specs=[_rb(Q, XBC), _rb(Q, 128), _vb(1, 128), _vb(1, 128), _vb(1, D_SSM)],
        out_specs=(_rb(Q, D_SSM), pl.BlockSpec((1, D_SSM, STATE), lambda i: (i, 0, 0))),
        scratch_shapes=[pltpu.VMEM((D_SSM, STATE), F32)],
        compiler_params=_cparams("arbitrary"))(xa, dtraw, dt_bias, a_log, dskip_x)


def _ssd_bwd(xa, dtraw, dt_bias, a_log, dskip_x, dy, states, dproj):
    L = xa.shape[0]
    nc = L // Q

    def body(xa_ref, dtr_ref, dtb_ref, alog_ref, dsk_ref, dy_ref, st_ref, dp_in,
             dxa_ref, ddt_ref, acc_ref, dska_ref, dS):
        del dp_in
        c = pl.program_id(0)

        @pl.when(c == 0)
        def _():
            dS[...] = jnp.zeros_like(dS)
            acc_ref[...] = jnp.zeros_like(acc_ref)
            dska_ref[...] = jnp.zeros_like(dska_ref)

        dt_in, dt, A, acum, last, tril = _chunk_decay(dtr_ref, dtb_ref, alog_ref)
        acum_t = acum.T
        e = jnp.exp(acum)
        w = jnp.exp(last - acum)
        e_last = jnp.exp(last)
        hid2 = lax.broadcasted_iota(jnp.int32, (Q, 128), 1) // HEAD_DIM
        triu = jnp.logical_not(tril) | (lax.broadcasted_iota(jnp.int32, (Q, Q), 0) == lax.broadcasted_iota(jnp.int32, (Q, Q), 1))
        per_head_hi, per_head_lo = _split2_rows(dt, e, w)
        rid = lax.broadcasted_iota(jnp.int32, (GW, STATE), 0) // HEAD_DIM
        rid1 = lax.broadcasted_iota(jnp.int32, (GW, 1), 0) // HEAD_DIM
        lane = lax.broadcasted_iota(jnp.int32, (Q, 128), 1)
        sub = lax.broadcasted_iota(jnp.int32, (128, Q), 0)
        lane1 = lax.broadcasted_iota(jnp.int32, (1, 128), 1)
        segr = lax.broadcasted_iota(jnp.int32, (GW, 128), 0) // HEAD_DIM
        segl = lax.broadcasted_iota(jnp.int32, (GW, 128), 1)
        dacum = jnp.zeros((Q, 128), F32)
        dacum_t = jnp.zeros((128, Q), F32)
        ddt = jnp.zeros((Q, 128), F32)
        dlast = jnp.zeros((1, 128), F32)
        for g in range(GROUPS):
            h0 = 4 * g
            bcol = D_SSM + STATE * g
            ccol = D_SSM + GROUPS * STATE + STATE * g
            bg = _silu(xa_ref[:, bcol:bcol + STATE]).astype(BF16)
            cg = _silu(xa_ref[:, ccol:ccol + STATE]).astype(BF16)
            sg = st_ref[0, GW * g:GW * (g + 1), :]
            sgb = sg.astype(BF16)
            dsg = dS[GW * g:GW * (g + 1), :]
            dsgb = dsg.astype(BF16)
            xg = _silu(xa_ref[:, GW * g:GW * (g + 1)])
            dyg = dy_ref[:, GW * g:GW * (g + 1)]
            spread = _expand_heads(per_head_hi, per_head_lo, h0)
            dtx, ex, wx = spread[0:Q], spread[Q:2 * Q], spread[2 * Q:3 * Q]
            el = jnp.where(rid == 0, e_last[:, h0:h0 + 1],
                           jnp.where(rid == 1, e_last[:, h0 + 1:h0 + 2],
                                     jnp.where(rid == 2, e_last[:, h0 + 2:h0 + 3], e_last[:, h0 + 3:h0 + 4])))
            eseg = (segr + h0 == segl).astype(BF16)
            xdt = xg * dtx
            xb = xdt.astype(BF16)
            xw = xdt * wx
            xwb = xw.astype(BF16)
            cb = _dot(cg, bg, NT)
            cb_t = _dot(bg, cg, NT)
            cs = _dot(cg, sgb, NT)
            dye = dyg * ex
            dyeb = dye.astype(BF16)
            dc = _dot(dyeb, sgb, NN)
            ds_prev = el * dsg + _dot(dyeb, cg, TN)
            bds = _dot(bg, dsgb, NT)
            dx = wx * bds
            dcb = jnp.zeros((Q, Q), F32)
            dxp = [jnp.zeros((Q, 128), F32), jnp.zeros((Q, 128), F32)]
            for r in range(4):
                h = h0 + r
                sl = slice(128 * (r // 2), 128 * (r // 2 + 1))
                seg = acum[:, h:h + 1] - acum_t[h:h + 1, :]
                lm = jnp.where(tril, jnp.exp(jnp.minimum(seg, 0.0)), 0.0)
                mf = cb * lm
                dyr = jnp.where(hid2 == r % 2, dyg[:, sl], 0.0).astype(BF16)
                dm = _dot(dyr, xb[:, sl], NT)
                dcb = dcb + dm * lm
                nh = dm * mf
                dacum = dacum + jnp.where(lane == h, jnp.sum(nh, axis=1, keepdims=True), 0.0)
                dacum_t = dacum_t + jnp.where(sub == h, jnp.sum(nh, axis=0, keepdims=True), 0.0)
                mt = cb_t * jnp.where(triu, jnp.exp(jnp.minimum(-seg, 0.0)), 0.0)
                dxp[r // 2] = dxp[r // 2] + _dot(mt.astype(BF16), dyr, NN)
            dx = dx + jnp.concatenate(dxp, axis=1)
            dcbb = dcb.astype(BF16)
            db = _dot(dcbb, cg, TN) + _dot(xwb, dsgb, NN)
            dc = dc + _dot(dcbb, bg, NN)
            t2 = xw * bds
            dacum = dacum + _dot2(dye * cs - t2, eseg)
            t2c = jnp.broadcast_to(jnp.sum(t2, axis=0, keepdims=True), (8, GW))
            dlast = dlast + _dot2(t2c, eseg)[0:1, :]
            v = jnp.sum(dsg * sg * el, axis=1, keepdims=True)
            for r in range(4):
                s_r = jnp.sum(jnp.where(rid1 == r, v, 0.0), axis=0, keepdims=True)
                dlast = dlast + jnp.where(lane1 == h0 + r, s_r, 0.0)
            ddt = ddt + _dot((dx * xg).astype(BF16), eseg, NN)
            dxa_ref[:, GW * g:GW * (g + 1)] = dx * dtx + dyg * dsk_ref[:, GW * g:GW * (g + 1)]
            dxa_ref[:, bcol:bcol + STATE] = db
            dxa_ref[:, ccol:ccol + STATE] = dc
            dska_ref[:, GW * g:GW * (g + 1)] += jnp.sum(dyg * xg, axis=0, keepdims=True)
            dS[GW * g:GW * (g + 1), :] = ds_prev
        dac = dacum - dacum_t.T
        rowq = lax.broadcasted_iota(jnp.int32, (Q, 128), 0)
        dac = dac + jnp.where(rowq == Q - 1, dlast, 0.0)
        ti = lax.broadcasted_iota(jnp.int32, (Q, Q), 0)
        si = lax.broadcasted_iota(jnp.int32, (Q, Q), 1)
        da = _dot_exact_lhs01((si >= ti).astype(BF16), dac)
        ddt_tot = ddt + da * A
        ddtraw = jnp.where(lane < HEADS, ddt_tot * _sigmoid(dt_in), 0.0)
        acc_ref[0:1, :] += jnp.sum(da * dt, axis=0, keepdims=True) * A
        acc_ref[1:2, :] += jnp.sum(ddtraw, axis=0, keepdims=True)
        ddt_ref[:, 0:128] = ddtraw.astype(BF16)
        ddt_ref[:, 128:CBLK] = jnp.zeros((Q, CBLK - 128), BF16)

    rev = lambda i: (nc - 1 - i, 0)
    return pl.pallas_call(
        body, name="ssd_bwd", grid=(nc,),
        out_shape=(jax.ShapeDtypeStruct((L, XBC), F32), jax.ShapeDtypeStruct((L, N_PAD), BF16),
                   jax.ShapeDtypeStruct((8, 128), F32), jax.ShapeDtypeStruct((1, D_SSM), F32)),
        in_specs=[pl.BlockSpec((Q, XBC), rev), pl.BlockSpec((Q, 128), rev), _vb(1, 128), _vb(1, 128), _vb(1, D_SSM),
                  pl.BlockSpec((Q, D_SSM), rev), pl.BlockSpec((1, D_SSM, STATE), lambda i: (nc - 1 - i, 0, 0)),
                  pl.BlockSpec(memory_space=pl.ANY)],
        out_specs=(pl.BlockSpec((Q, XBC), rev), pl.BlockSpec((Q, CBLK), lambda i: (nc - 1 - i, DT_BLOCK)),
                   _vb(8, 128), _vb(1, D_SSM)),
        input_output_aliases={7: 1},
        scratch_shapes=[pltpu.VMEM((D_SSM, STATE), F32)],
        compiler_params=_cparams("arbitrary"))(xa, dtraw, dt_bias, a_log, dskip_x, dy, states, dproj)


def _ssm_post_fwd(y, proj, nw):
    L = y.shape[0]
    tm = _row_tile(L)

    def body(y_ref, z_ref, nw_ref, o_ref):
        yz = y_ref[...] * _silu(z_ref[...])
        for k in range(CBLK // GW):
            s = yz[:, GW * k:GW * (k + 1)]
            rg = lax.rsqrt(jnp.mean(s * s, axis=-1, keepdims=True) + EPS)
            o_ref[:, GW * k:GW * (k + 1)] = (s * rg * nw_ref[:, GW * k:GW * (k + 1)]).astype(BF16)

    blk = pl.BlockSpec((tm, CBLK), lambda j, i: (i, j))
    return pl.pallas_call(
        body, name="ssm_post_fwd", grid=(2, L // tm), out_shape=jax.ShapeDtypeStruct((L, D_SSM), BF16),
        in_specs=[blk, pl.BlockSpec((tm, CBLK), lambda j, i: (i, Z_BLOCK0 + j)), pl.BlockSpec((1, CBLK), lambda j, i: (0, j))],
        out_specs=blk, compiler_params=_cparams("parallel", "parallel"))(y, proj, nw)


def _ssm_post_bwd(dyn, y, proj, nw, dproj):
    L = y.shape[0]
    tm = _row_tile(L)

    def body(dyn_ref, y_ref, z_ref, nw_ref, dp_in, dy_ref, dz_ref, acc_ref):
        del dp_in
        i = pl.program_id(1)

        @pl.when(i == 0)
        def _():
            acc_ref[...] = jnp.zeros_like(acc_ref)

        z = z_ref[...]
        yv = y_ref[...]
        sz = _silu(z)
        yz = yv * sz
        dyn_v = dyn_ref[...]
        for k in range(CBLK // GW):
            sl = slice(GW * k, GW * (k + 1))
            s = yz[:, sl]
            rg = lax.rsqrt(jnp.mean(s * s, axis=-1, keepdims=True) + EPS)
            yhat = s * rg
            dn = dyn_v[:, sl]
            acc_ref[0:1, sl] += jnp.sum(dn * yhat, axis=0, keepdims=True)
            dyhat = dn * nw_ref[:, sl]
            dyz = rg * (dyhat - yhat * jnp.mean(dyhat * yhat, axis=-1, keepdims=True))
            dy_ref[:, sl] = dyz * sz[:, sl]
            dz_ref[:, sl] = (dyz * yv[:, sl] * _dsilu(z[:, sl])).astype(BF16)

    blk = pl.BlockSpec((tm, CBLK), lambda j, i: (i, j))
    zblk = pl.BlockSpec((tm, CBLK), lambda j, i: (i, Z_BLOCK0 + j))
    return pl.pallas_call(
        body, name="ssm_post_bwd", grid=(2, L // tm),
        out_shape=(jax.ShapeDtypeStruct((L, D_SSM), F32), jax.ShapeDtypeStruct((L, N_PAD), BF16),
                   jax.ShapeDtypeStruct((8, D_SSM), F32)),
        in_specs=[blk, blk, zblk, pl.BlockSpec((1, CBLK), lambda j, i: (0, j)), pl.BlockSpec(memory_space=pl.ANY)],
        out_specs=(blk, zblk, pl.BlockSpec((8, CBLK), lambda j, i: (0, j))),
        input_output_aliases={4: 1},
        compiler_params=_cparams("arbitrary", "arbitrary"))(dyn, y, proj, nw, dproj)


def _merge_fwd(proj, p_conv, p_ssm):
    L = proj.shape[0]
    tm = _row_tile(L)

    def body(gl_ref, pc_ref, ps_ref, o_ref):
        o_ref[...] = (_sigmoid(gl_ref[:, 0:D]) * pc_ref[...] + _sigmoid(gl_ref[:, D:2 * D]) * ps_ref[...]).astype(BF16)

    return pl.pallas_call(
        body, name="merge_fwd", grid=(L // tm,), out_shape=jax.ShapeDtypeStruct((L, D), BF16),
        in_specs=[_rb(tm, 2 * D, GL_BLOCK2K), _rb(tm, D), _rb(tm, D)], out_specs=_rb(tm, D),
        compiler_params=_cparams("parallel"))(proj, p_conv, p_ssm)


def _merge_bwd(dmerged, proj, p_conv, p_ssm):
    L = proj.shape[0]
    tm = _row_tile(L)

    def body(dm_ref, gl_ref, pc_ref, ps_ref, dpc_ref, dps_ref, dgl_ref):
        dm = dm_ref[...]
        sc = _sigmoid(gl_ref[:, 0:D])
        ss = _sigmoid(gl_ref[:, D:2 * D])
        dpc_ref[...] = (dm * sc).astype(BF16)
        dps_ref[...] = (dm * ss).astype(BF16)
        dgl_ref[:, 0:D] = (dm * pc_ref[...] * sc * (1.0 - sc)).astype(BF16)
        dgl_ref[:, D:2 * D] = (dm * ps_ref[...] * ss * (1.0 - ss)).astype(BF16)

    bf = jax.ShapeDtypeStruct((L, D), BF16)
    return pl.pallas_call(
        body, name="merge_bwd", grid=(L // tm,),
        out_shape=(bf, bf, jax.ShapeDtypeStruct((L, N_PAD), BF16)),
        in_specs=[_rb(tm, D), _rb(tm, 2 * D, GL_BLOCK2K), _rb(tm, D), _rb(tm, D)],
        out_specs=(_rb(tm, D), _rb(tm, D), _rb(tm, 2 * D, GL_BLOCK2K)),
        compiler_params=_cparams("parallel"))(dmerged, proj, p_conv, p_ssm)


def _conv_bwd(dyc, proj, conv_w8, dproj):
    L = proj.shape[0]
    tm = _row_tile(L)
    n8 = L // 8
    nt = L // tm

    def body(dy_ref, dyn_ref, cb_ref, cbn_ref, cc_ref, cch_ref, cx_ref, cxh_ref, w_ref, dp_in, d_ref, acc_ref):
        del dp_in
        i = pl.program_id(0)

        @pl.when(i == 0)
        def _():
            acc_ref[...] = jnp.zeros_like(acc_ref)

        cc, cx, cb = cc_ref[...], cx_ref[...], cb_ref[...]
        q = cc * cx
        qh = jnp.where(i > 0, cch_ref[...] * cxh_ref[...], 0.0)
        q1 = _shift_down(q, qh, 1)
        q2 = _shift_down(q, qh, 2)
        cq = w_ref[2:3, :] * q + w_ref[1:2, :] * q1 + w_ref[0:1, :] * q2
        dy = dy_ref[...]
        dcq = dy * cb
        dcqn = jnp.where(i < nt - 1, dyn_ref[...] * cbn_ref[...], 0.0)
        dcq1, dcq2 = _shift_up(dcq, dcqn, 1), _shift_up(dcq, dcqn, 2)
        dq = w_ref[2:3, :] * dcq + w_ref[1:2, :] * dcq1 + w_ref[0:1, :] * dcq2
        d_ref[:, 0:D] = (dy * cq).astype(BF16)
        d_ref[:, D:2 * D] = (dq * cx).astype(BF16)
        d_ref[:, 2 * D:3 * D] = (dq * cc).astype(BF16)
        acc_ref[2:3, :] += jnp.sum(dcq * q, axis=0, keepdims=True)
        acc_ref[1:2, :] += jnp.sum(dcq1 * q, axis=0, keepdims=True)
        acc_ref[0:1, :] += jnp.sum(dcq2 * q, axis=0, keepdims=True)

    return pl.pallas_call(
        body, name="conv_bwd", grid=(nt,),
        out_shape=(jax.ShapeDtypeStruct((L, N_PAD), BF16), jax.ShapeDtypeStruct((8, D), F32)),
        in_specs=[_rb(tm, D), _halo_next(tm, D, n8), _rb(tm, D, 0), _halo_next(tm, D, n8, 0),
                  _rb(tm, D, 1), _halo_prev(tm, D, 1), _rb(tm, D, 2), _halo_prev(tm, D, 2), _vb(8, D),
                  pl.BlockSpec(memory_space=pl.ANY)],
        out_specs=(_rb(tm, 3 * D, 0), _vb(8, D)),
        input_output_aliases={9: 0},
        compiler_params=_cparams("arbitrary"))(dyc, dyc, proj, proj, proj, proj, proj, proj, conv_w8, dproj)


def _ssm_pre_bwd(dxa, xc, proj, w8, dproj):
    L = proj.shape[0]
    tm = _row_tile(L)
    n8 = L // 8
    nt = L // tm

    def body(dxa_ref, dxan_ref, xc_ref, xcn_ref, x_ref, w_ref, dp_in, d_ref, acc_ref):
        del dp_in
        i = pl.program_id(1)

        @pl.when(i == 0)
        def _():
            acc_ref[...] = jnp.zeros_like(acc_ref)

        dxc = dxa_ref[...] * _dsilu(xc_ref[...])
        dxcn = jnp.where(i < nt - 1, dxan_ref[...] * _dsilu(xcn_ref[...]), 0.0)
        d1, d2, d3 = _shift_up(dxc, dxcn, 1), _shift_up(dxc, dxcn, 2), _shift_up(dxc, dxcn, 3)
        d_ref[...] = (w_ref[3:4, :] * dxc + w_ref[2:3, :] * d1 + w_ref[1:2, :] * d2 + w_ref[0:1, :] * d3).astype(BF16)
        x = x_ref[...]
        acc_ref[3:4, :] += jnp.sum(dxc * x, axis=0, keepdims=True)
        acc_ref[2:3, :] += jnp.sum(d1 * x, axis=0, keepdims=True)
        acc_ref[1:2, :] += jnp.sum(d2 * x, axis=0, keepdims=True)
        acc_ref[0:1, :] += jnp.sum(d3 * x, axis=0, keepdims=True)
        acc_ref[4:5, :] += jnp.sum(dxc, axis=0, keepdims=True)

    blk = pl.BlockSpec((tm, CBLK), lambda j, i: (i, j))
    nxt = pl.BlockSpec((8, CBLK), lambda j, i: (jnp.minimum((i + 1) * (tm // 8), n8 - 1), j))
    pblk = pl.BlockSpec((tm, CBLK), lambda j, i: (i, _xbc_block(j)))
    return pl.pallas_call(
        body, name="ssm_pre_bwd", grid=(4, nt),
        out_shape=(jax.ShapeDtypeStruct((L, N_PAD), BF16), jax.ShapeDtypeStruct((8, XBC), F32)),
        in_specs=[blk, nxt, blk, nxt, pblk,
                  pl.BlockSpec((8, CBLK), lambda j, i: (0, j)), pl.BlockSpec(memory_space=pl.ANY)],
        out_specs=(pblk, pl.BlockSpec((8, CBLK), lambda j, i: (0, j))),
        input_output_aliases={6: 0},
        compiler_params=_cparams("arbitrary", "arbitrary"))(dxa, dxa, xc, xc, proj, w8, dproj)


def _loss_head(h1, dn, gate2, fnorm, target):
    L = h1.shape[0]
    tm = _row_tile(L)

    def body(h_ref, dn_ref, g_ref, fn_ref, t_ref, dh_ref, ddn_ref, acc_ref):
        i = pl.program_id(0)

        @pl.when(i == 0)
        def _():
            acc_ref[...] = jnp.zeros_like(acc_ref)

        dnv = dn_ref[...]
        g = g_ref[...]
        x = h_ref[...] + g * dnv
        r = lax.rsqrt(jnp.mean(x * x, axis=-1, keepdims=True) + EPS)
        xhat = x * r
        diff = xhat * fn_ref[...] - t_ref[...]
        dy = diff * (1.0 / D)
        dxhat = dy * fn_ref[...]
        dh = r * (dxhat - xhat * jnp.mean(dxhat * xhat, axis=-1, keepdims=True))
        dh_ref[...] = dh
        ddn_ref[...] = (dh * g).astype(BF16)
        acc_ref[0:1, :] += jnp.sum(dy * xhat, axis=0, keepdims=True)
        acc_ref[1:2, :] += jnp.sum(dh * dnv, axis=0, keepdims=True)
        acc_ref[2:3, :] += 0.5 * jnp.sum(jnp.mean(diff * diff, axis=-1, keepdims=True), axis=0, keepdims=True)

    row, vec = _rb(tm, D), _vb(1, D)
    return pl.pallas_call(
        body, name="loss_head", grid=(L // tm,),
        out_shape=(jax.ShapeDtypeStruct((L, D), F32), jax.ShapeDtypeStruct((L, D), BF16), jax.ShapeDtypeStruct((8, D), F32)),
        in_specs=[row, row, vec, vec, row], out_specs=(row, row, _vb(8, D)),
        compiler_params=_cparams("arbitrary"))(h1, dn, gate2, fnorm, target)


def _norm_bwd(du, h, dh_in, ln, scale, br, gate, name):
    L = h.shape[0]
    tm = _row_tile(L)
    has_br = br is not None

    def body(*refs):
        if has_br:
            du_ref, h_ref, dhi_ref, ln_ref, sc_ref, br_ref, g_ref, dh_ref, dbr_ref, acc_ref = refs
        else:
            du_ref, h_ref, dhi_ref, ln_ref, sc_ref, dh_ref, acc_ref = refs
        i = pl.program_id(0)

        @pl.when(i == 0)
        def _():
            acc_ref[...] = jnp.zeros_like(acc_ref)

        x = h_ref[...]
        duv = du_ref[...]
        r = lax.rsqrt(jnp.mean(x * x, axis=-1, keepdims=True) + EPS)
        xhat = x * r
        dn = duv * (1.0 + sc_ref[...])
        dxhat = dn * ln_ref[...]
        dh = dhi_ref[...] + r * (dxhat - xhat * jnp.mean(dxhat * xhat, axis=-1, keepdims=True))
        dh_ref[...] = dh
        acc_ref[0:1, :] += jnp.sum(duv, axis=0, keepdims=True)
        acc_ref[1:2, :] += jnp.sum(duv * xhat * ln_ref[...], axis=0, keepdims=True)
        acc_ref[2:3, :] += jnp.sum(dn * xhat, axis=0, keepdims=True)
        if has_br:
            dbr_ref[...] = (dh * g_ref[...]).astype(BF16)
            acc_ref[3:4, :] += jnp.sum(dh * br_ref[...], axis=0, keepdims=True)

    row, vec = _rb(tm, D), _vb(1, D)
    f32o, acc = jax.ShapeDtypeStruct((L, D), F32), jax.ShapeDtypeStruct((8, D), F32)
    if has_br:
        return pl.pallas_call(
            body, name=name, grid=(L // tm,), out_shape=(f32o, jax.ShapeDtypeStruct((L, D), BF16), acc),
            in_specs=[row, row, row, vec, vec, row, vec], out_specs=(row, row, _vb(8, D)),
            compiler_params=_cparams("arbitrary"))(du, h, dh_in, ln, scale, br, gate)
    return pl.pallas_call(
        body, name=name, grid=(L // tm,), out_shape=(f32o, acc),
        in_specs=[row, row, row, vec, vec], out_specs=(row, _vb(8, D)),
        compiler_params=_cparams("arbitrary"))(du, h, dh_in, ln, scale)


def _layer_fwd(h_prev, br_prev, gate_prev, mod, sp, W, next_shards=None):
    s = {}
    tag = ""
    if br_prev is None:
        s["h0"] = h_prev
        s["u"] = _resid_norm(h_prev, None, None, sp["ln1"], mod[1], mod[0], "norm1" + tag)
    else:
        s["h0"], s["u"] = _resid_norm(h_prev, br_prev, gate_prev, sp["ln1"], mod[1], mod[0], "norm1" + tag)
    if next_shards is None:
        s["proj"] = _mm(s["u"], W["w_full"], "nn", n_out=N_MAIN, name="mm_proj" + tag)
    else:
        s["proj"], *s["next_gathered"] = _mm(s["u"], W["w_full"], "nn", n_out=N_MAIN, name="mm_proj_wgather",
                                             side=_gather_side(next_shards))
    s["dtraw"] = _mm(s["u"], W["w_full"], "nn", tn=128, n_out=128, b_col0=N_MAIN // 128, name="mm_dt" + tag)
    s["y_conv"] = _conv_fwd(s["proj"], sp["conv_w8"])
    s["p_conv"] = _mm(s["y_conv"], W["w_conv_out"], "nn", name="mm_pconv" + tag)
    s["xc"] = _ssm_pre_fwd(s["proj"], sp["ssm_conv_w8"], sp["ssm_conv_b"])
    s["y"], s["states"] = _ssd_fwd(s["xc"], s["dtraw"], sp["dt_bias"], sp["a_log"], sp["dskip_x"])
    s["yn"] = _ssm_post_fwd(s["y"], s["proj"], sp["ssm_norm_w"])
    s["p_ssm"] = _mm(s["yn"], W["w_ssm_out"], "nn", name="mm_pssm" + tag)
    s["merged"] = _merge_fwd(s["proj"], s["p_conv"], s["p_ssm"])
    s["mix"] = _mm(s["merged"], W["w_o"], "nn", name="mm_mix" + tag)
    s["h1"], s["u2"] = _resid_norm(s["h0"], s["mix"], mod[2], sp["ln2"], mod[4], mod[3], "norm2" + tag)
    s["a_up"], s["hid"] = _mm(s["u2"], W["w_up"], "nn", epi="relu2", b_cm=True, name="mm_up" + tag)
    s["dn"] = _mm(s["hid"], W["w_down"], "nn", name="mm_down" + tag)
    return s


def _layer_bwd(s, dh2, ddn, mod, sp, W, br_below, gate_below, rs_parts=None):
    tag = ""
    g = {}
    da_up = _mm(ddn, W["w_down"], "nt", out_dtype=BF16, epi="relu2_bwd", extra=s["a_up"], name="mm_dhid" + tag)
    g["w_down"] = _mm(s["hid"], ddn, "tn", name="mm_gdown" + tag)
    du2 = _mm(da_up, W["w_up"], "nt", b_cm=True, name="mm_du2" + tag)
    g["w_up"] = _mm(s["u2"], da_up, "tn", out_cm=True, name="mm_gup" + tag)
    dh1, dmix, acc2 = _norm_bwd(du2, s["h1"], dh2, sp["ln2"], mod[4], s["mix"], mod[2], "norm2_bwd" + tag)
    dmerged = _mm(dmix, W["w_o"], "nt", name="mm_dmerged" + tag)
    g["w_o"] = _mm(s["merged"], dmix, "tn", name="mm_go" + tag)
    dpc, dps, dproj = _merge_bwd(dmerged, s["proj"], s["p_conv"], s["p_ssm"])
    dyc = _mm(dpc, W["w_conv_out"], "nt", name="mm_dyconv" + tag)
    g["w_conv_out"] = _mm(s["y_conv"], dpc, "tn", name="mm_gconvout" + tag)
    dyn = _mm(dps, W["w_ssm_out"], "nt", name="mm_dyn" + tag)
    g["w_ssm_out"] = _mm(s["yn"], dps, "tn", name="mm_gssmout" + tag)
    dproj, conv_acc = _conv_bwd(dyc, s["proj"], sp["conv_w8"], dproj)
    dy, dproj, post_acc = _ssm_post_bwd(dyn, s["y"], s["proj"], sp["ssm_norm_w"], dproj)
    dxa, dproj, ssd_acc, dsk_acc = _ssd_bwd(s["xc"], s["dtraw"], sp["dt_bias"], sp["a_log"], sp["dskip_x"], dy,
                                            s["states"], dproj)
    dproj, pre_acc = _ssm_pre_bwd(dxa, s["xc"], s["proj"], sp["ssm_conv_w8"], dproj)
    g["w_full"] = _mm(s["u"], dproj, "tn", name="mm_gin" + tag)
    big = _big_grads(g)
    rs_recv = None
    if rs_parts is None:
        du, *from_sib = _mm(dproj, W["w_full"], "nt", name="mm_du_pair", side=_pair_side([big[k] for k in BIG]))
    else:
        xs = _xchip_side(rs_parts)
        du, *both = _mm(dproj, W["w_full"], "nt", name="mm_du_gradrs",
                        side=_join_sides(xs, _pair_side([big[k] for k in BIG], sem0=xs[4])))
        rs_recv, from_sib = both[:NBIG], both[NBIG:]
    if br_below is None:
        dh0, acc1 = _norm_bwd(du, s["h0"], dh1, sp["ln1"], mod[1], None, None, "norm1_bwd" + tag)
        ddn_below = None
    else:
        dh0, ddn_below, acc1 = _norm_bwd(du, s["h0"], dh1, sp["ln1"], mod[1], br_below, gate_below, "norm1_bwd" + tag)
    accs = dict(acc1=acc1, acc2=acc2, conv=conv_acc, post=post_acc, ssd=ssd_acc, dsk=dsk_acc, pre=pre_acc)
    return big, from_sib, dh0, ddn_below, accs, rs_recv


def _prep_layer_weights(w):
    w_in = jnp.transpose(w["w_in"], (1, 0, 2)).reshape(D, N_PROJ)
    cols = [w_in[:, CBLK * b:CBLK * (b + 1)] for b in ORIG_BLOCK_ORDER]
    cols += [w_in[:, N_MAIN:], jnp.zeros((D, N_PAD - N_PROJ), w_in.dtype)]
    return dict(w_full=jnp.concatenate(cols, axis=1), w_conv_out=w["w_conv_out"].reshape(D, D),
                w_ssm_out=w["w_ssm_out"].reshape(D_SSM, D), w_o=w["w_o"].reshape(D, D), w_up=w["w_up"],
                w_down=w["w_down"].reshape(4 * D, D))


def _prep_small(p):
    pad8 = lambda a: jnp.pad(a, ((0, 8 - a.shape[0]), (0, 0)))
    pad128 = lambda a: jnp.pad(a, (0, 128 - a.shape[0]))[None, :]
    return dict(ln1=p["ln1"][None, :], ln2=p["ln2"][None, :], conv_w8=pad8(p["conv_w"]),
                ssm_conv_w8=pad8(p["ssm_conv_w"]), ssm_conv_b=p["ssm_conv_b"][None, :],
                dt_bias=pad128(p["dt_bias"]), a_log=pad128(p["a_log"]),
                dskip_x=jnp.repeat(p["d_skip"], HEAD_DIM)[None, :], ssm_norm_w=p["ssm_norm_w"][None, :])


def _unpermute_w_in_grad(gfull):
    inv = [ORIG_BLOCK_ORDER.index(b) for b in range(len(ORIG_BLOCK_ORDER))]
    cols = [gfull[:, CBLK * i:CBLK * (i + 1)] for i in inv]
    cols.append(gfull[:, N_MAIN:N_MAIN + (N_PROJ - N_MAIN)])
    return jnp.transpose(jnp.concatenate(cols, axis=1).reshape(D, 4, N_PROJ // 4), (1, 0, 2))


def _big_grads(g):
    return dict(w_in=_unpermute_w_in_grad(g["w_full"]), w_conv_out=g["w_conv_out"].reshape(4, D // 4, D),
                w_ssm_out=g["w_ssm_out"].reshape(4, D_SSM // 4, D), w_o=g["w_o"].reshape(4, D // 4, D),
                w_up=g["w_up"], w_down=g["w_down"].reshape(4, D, D))


def _collect_layer_grads(big, accs):
    out = dict(big)
    out["ln1"] = accs["acc1"][2]
    out["ln2"] = accs["acc2"][2]
    out["conv_w"] = accs["conv"][0:3]
    out["ssm_conv_w"] = accs["pre"][0:4]
    out["ssm_conv_b"] = accs["pre"][4]
    out["a_log"] = accs["ssd"][0, :HEADS]
    out["dt_bias"] = accs["ssd"][1, :HEADS]
    out["d_skip"] = accs["dsk"].reshape(HEADS, HEAD_DIM).sum(axis=-1)
    out["ssm_norm_w"] = accs["post"][0]
    out["dmod"] = [accs["acc1"][0], accs["acc1"][1], accs["acc2"][3], accs["acc2"][0], accs["acc2"][1]]
    return out


ANY = pl.BlockSpec(memory_space=pl.ANY)


def _my_pos():
    return lax.axis_index("x"), lax.axis_index("y"), lax.axis_index("c")


def _all_gather_small(x_shard, name):
    m_per, n = x_shard.shape

    def body(x_ref, out_ref, send_sems, recv_sems, local_sem):
        x, y, c = _my_pos()
        me, sibling = (x, y, c), (x, y, 1 - c)
        chips = [(1 - x, y), (x, 1 - y), (1 - x, 1 - y)]

        def rows(px, py, pc):
            return out_ref.at[pl.ds((4 * px + 2 * py + pc) * m_per, m_per), :]

        def copy(k, block, to, src=None):
            return pltpu.make_async_remote_copy(
                src_ref=rows(*block) if src is None else src, dst_ref=rows(*block),
                send_sem=send_sems.at[k], recv_sem=recv_sems.at[k], device_id=to, device_id_type=MESH)

        mine = pltpu.make_async_copy(x_ref, rows(*me), local_sem)
        mine.start()
        first = [copy(0, me, sibling, src=x_ref)]
        first += [copy(1 + j, me, (*chip, c), src=x_ref) for j, chip in enumerate(chips)]
        for cp in first:
            cp.start()
        passed = [copy(4 + j, (*chip, c), sibling) for j, chip in enumerate(chips)]
        for j, chip in enumerate(chips):
            copy(1 + j, (*chip, c), me).wait_recv()
            passed[j].start()
        copy(0, sibling, me).wait_recv()
        for j, chip in enumerate(chips):
            copy(4 + j, (*chip, 1 - c), me).wait_recv()
        for cp in first + passed:
            cp.wait_send()
        mine.wait()

    return pl.pallas_call(
        body, name=name, out_shape=jax.ShapeDtypeStruct((8 * m_per, n), x_shard.dtype),
        in_specs=[pl.BlockSpec(memory_space=pltpu.VMEM)], out_specs=pl.BlockSpec(memory_space=pltpu.VMEM),
        scratch_shapes=[pltpu.SemaphoreType.DMA((7,)), pltpu.SemaphoreType.DMA((7,)), pltpu.SemaphoreType.DMA],
        compiler_params=pltpu.CompilerParams(vmem_limit_bytes=VMEM_LIMIT_BYTES),
    )(x_shard)


BIG = ("w_in", "w_conv_out", "w_ssm_out", "w_o", "w_up", "w_down")
NBIG = len(BIG)


def _half(ref, c, lead):
    half = ref.shape[lead] // 2
    idx = (slice(None),) * lead + (pl.ds(c * half, half),)
    return ref.at[idx]


def _gather_weights(shards, name):
    n = len(shards)
    start, finish, _, out_shape, n_sems = _gather_side(shards)

    def body(*refs):
        side_refs = (refs[:n], refs[n:2 * n], refs[2 * n], refs[2 * n + 1])
        start(*side_refs)
        finish(*side_refs)

    return pl.pallas_call(
        body, name=name, out_shape=out_shape, in_specs=[ANY] * n, out_specs=tuple([ANY] * n),
        scratch_shapes=[pltpu.SemaphoreType.DMA((n_sems,)), pltpu.SemaphoreType.DMA((n_sems,))],
    )(*shards)


def _gather_side(shards):
    n = len(shards)

    def copies(srcs, outs, send_sems, recv_sems):
        x, y, c = _my_pos()
        chips = [(1 - x, y), (x, 1 - y), (1 - x, 1 - y)]

        def ici(w, j, slot):
            px, py = chips[j]
            return pltpu.make_async_remote_copy(
                src_ref=_half(srcs[w], c, 0), dst_ref=_half(outs[w].at[slot], c, 0),
                send_sem=send_sems.at[6 * w + j], recv_sem=recv_sems.at[6 * w + j],
                device_id=(px, py, c), device_id_type=MESH)

        def d2d(w, j, core):
            px, py = chips[j]
            piece = _half(outs[w].at[2 * px + py], core, 0)
            return pltpu.make_async_remote_copy(
                src_ref=piece, dst_ref=piece, send_sem=send_sems.at[6 * w + 3 + j], recv_sem=recv_sems.at[6 * w + 3 + j],
                device_id=(x, y, 1 - c), device_id_type=MESH)

        return ici, d2d, 2 * x + y, [2 * px + py for px, py in chips], c

    def start(*side_refs):
        ici, _, chip, _, _ = copies(*side_refs)
        for w in range(n):
            for j in range(3):
                ici(w, j, chip).start()

    def finish(*side_refs):
        ici, d2d, chip, peer_chips, c = copies(*side_refs)
        for w in range(n):
            for j in range(3):
                ici(w, j, peer_chips[j]).wait_recv()
                d2d(w, j, c).start()
        for w in range(n):
            for j in range(3):
                d2d(w, j, 1 - c).wait_recv()
        for w in range(n):
            for j in range(3):
                ici(w, j, chip).wait_send()
                d2d(w, j, c).wait_send()

    out_shape = tuple(jax.ShapeDtypeStruct((4,) + tuple(s.shape), s.dtype) for s in shards)
    return start, finish, list(shards), out_shape, 6 * n


def _pair_side(grads, sem0=0):
    n = len(grads)

    def copies(srcs, outs, send_sems, recv_sems):
        x, y, c = _my_pos()
        return [pltpu.make_async_remote_copy(
            src_ref=_half(srcs[w], 1 - c, 1), dst_ref=outs[w], send_sem=send_sems.at[sem0 + w],
            recv_sem=recv_sems.at[sem0 + w], device_id=(x, y, 1 - c), device_id_type=MESH) for w in range(n)]

    def start(*side_refs):
        for cp in copies(*side_refs):
            cp.start()

    def finish(*side_refs):
        for cp in copies(*side_refs):
            cp.wait()

    out_shape = tuple(jax.ShapeDtypeStruct((4, g.shape[1] // 2, g.shape[2]), g.dtype) for g in grads)
    return start, finish, list(grads), out_shape, n


def _join_sides(a, b):
    na = len(a[2])

    def start(srcs, outs, send_sems, recv_sems):
        a[0](srcs[:na], outs[:na], send_sems, recv_sems)
        b[0](srcs[na:], outs[na:], send_sems, recv_sems)

    def finish(srcs, outs, send_sems, recv_sems):
        a[1](srcs[:na], outs[:na], send_sems, recv_sems)
        b[1](srcs[na:], outs[na:], send_sems, recv_sems)

    return start, finish, a[2] + b[2], tuple(a[3]) + tuple(b[3]), a[4] + b[4]


def _xchip_exchange(parts, name):
    n = len(parts)
    start, finish, _, out_shape, n_sems = _xchip_side(parts)

    def body(*refs):
        side_refs = (refs[:n], refs[n:2 * n], refs[2 * n], refs[2 * n + 1])
        start(*side_refs)
        finish(*side_refs)

    return pl.pallas_call(
        body, name=name, out_shape=out_shape, in_specs=[ANY] * n, out_specs=tuple([ANY] * n),
        scratch_shapes=[pltpu.SemaphoreType.DMA((n_sems,)), pltpu.SemaphoreType.DMA((n_sems,))],
    )(*parts)


def _xchip_side(parts):
    n = len(parts)

    def copies(srcs, outs, send_sems, recv_sems):
        x, y, c = _my_pos()
        chips = [(1 - x, y), (x, 1 - y), (1 - x, 1 - y)]

        def copy(w, j, slot):
            px, py = chips[j]
            return pltpu.make_async_remote_copy(
                src_ref=srcs[w].at[2 * px + py], dst_ref=outs[w].at[slot],
                send_sem=send_sems.at[3 * w + j], recv_sem=recv_sems.at[3 * w + j],
                device_id=(px, py, c), device_id_type=MESH)

        return copy, 2 * x + y, [2 * px + py for px, py in chips]

    def start(*side_refs):
        copy, chip, _ = copies(*side_refs)
        for w in range(n):
            for j in range(3):
                copy(w, j, chip).start()

    def finish(*side_refs):
        copy, chip, peer_chips = copies(*side_refs)
        for w in range(n):
            for j in range(3):
                copy(w, j, peer_chips[j]).wait_recv()
        for w in range(n):
            for j in range(3):
                copy(w, j, chip).wait_send()

    out_shape = tuple(jax.ShapeDtypeStruct(p.shape, p.dtype) for p in parts)
    return start, finish, list(parts), out_shape, 3 * n


def _sib_exchange(parts, name):
    n = len(parts)

    def body(*refs):
        srcs, outs = refs[:n], refs[n:2 * n]
        send_sems, recv_sems = refs[2 * n], refs[2 * n + 1]
        x, y, c = _my_pos()
        cps = [pltpu.make_async_remote_copy(
            src_ref=srcs[w], dst_ref=outs[w], send_sem=send_sems.at[w], recv_sem=recv_sems.at[w],
            device_id=(x, y, 1 - c), device_id_type=MESH) for w in range(n)]
        for cp in cps:
            cp.start()
        for cp in cps:
            cp.wait()

    return pl.pallas_call(
        body, name=name, out_shape=tuple(jax.ShapeDtypeStruct(p.shape, p.dtype) for p in parts),
        in_specs=[ANY] * n, out_specs=tuple([ANY] * n),
        scratch_shapes=[pltpu.SemaphoreType.DMA((n,)), pltpu.SemaphoreType.DMA((n,))],
    )(*parts)


def _sum_tile(R, C, nbuf):
    for cand in (512, 256, 128, 64, 32, 16):
        if R % cand == 0 and cand * C * 4 * nbuf <= 12 * 1024 * 1024:
            return cand
    return R


def _pair_sum(g, r, core, name):
    n, R, C = g.shape
    half = R // 2
    tr = _sum_tile(half, C, 3)
    nb = half // tr

    def body(c_ref, g_ref, r_ref, o_ref):
        del c_ref
        o_ref[...] = (g_ref[...] + r_ref[...]).astype(BF16)

    return pl.pallas_call(
        body, name=name, out_shape=jax.ShapeDtypeStruct((n, half, C), BF16),
        grid_spec=pltpu.PrefetchScalarGridSpec(
            num_scalar_prefetch=1, grid=(n, nb),
            in_specs=[pl.BlockSpec((None, tr, C), lambda s, i, c_ref: (s, c_ref[0] * nb + i, 0)),
                      pl.BlockSpec((None, tr, C), lambda s, i, c_ref: (s, i, 0))],
            out_specs=pl.BlockSpec((None, tr, C), lambda s, i, c_ref: (s, i, 0))),
        compiler_params=_cparams("parallel", "parallel"))(core, g, r)


def _sum_lead(x, name):
    n, R, C = x.shape
    tr = _sum_tile(R, C, n + 1)

    def body(x_ref, o_ref):
        acc = x_ref[0].astype(F32)
        for s in range(1, n):
            acc = acc + x_ref[s].astype(F32)
        o_ref[...] = acc

    return pl.pallas_call(
        body, name=name, out_shape=jax.ShapeDtypeStruct((R, C), F32), grid=(R // tr,),
        in_specs=[pl.BlockSpec((n, tr, C), lambda i: (0, i, 0))], out_specs=pl.BlockSpec((tr, C), lambda i: (i, 0)),
        compiler_params=_cparams("parallel"))(x)


HIGHEST = lax.Precision.HIGHEST


def _ada_fwd(c_all, w_ada, b_sh):
    nl, _, cols = w_ada.shape

    def body(c_ref, w_ref, b_ref, o_ref):
        o_ref[0] = _dot(_silu(c_ref[...]), w_ref[0], NN, HIGHEST) + b_ref[0]

    return pl.pallas_call(
        body, name="ada_fwd", grid=(nl,), out_shape=jax.ShapeDtypeStruct((nl, 8, cols), F32),
        in_specs=[_vb(8, D), pl.BlockSpec((1, D, cols), lambda l: (l, 0, 0)), pl.BlockSpec((1, 1, cols), lambda l: (l, 0, 0))],
        out_specs=pl.BlockSpec((1, 8, cols), lambda l: (l, 0, 0)), compiler_params=_cparams("parallel"))(c_all, w_ada, b_sh)


def _ada_bwd(c_pad, dmod_sh):
    nl, _, cols = dmod_sh.shape

    def body(c_ref, d_ref, o_ref):
        o_ref[0] = _dot(_silu(c_ref[...]), d_ref[0], TN, HIGHEST)

    return pl.pallas_call(
        body, name="ada_bwd", grid=(nl,), out_shape=jax.ShapeDtypeStruct((nl, D, cols), F32),
        in_specs=[_vb(128, D), pl.BlockSpec((1, 128, cols), lambda l: (l, 0, 0))],
        out_specs=pl.BlockSpec((1, D, cols), lambda l: (l, 0, 0)), compiler_params=_cparams("parallel"))(c_pad, dmod_sh)


def _adamw(w, g, m, v, name):
    R, C = w.shape
    tr = R
    for cand in (512, 256, 128, 64, 32, 16, 8):
        if R % cand == 0 and cand * C * 4 <= 2 * 1024 * 1024:
            tr = cand
            break

    def body(w_ref, g_ref, m_ref, v_ref, d_ref, m2_ref, v2_ref):
        gv = g_ref[...]
        m2 = ADAM_B1 * m_ref[...] + (1.0 - ADAM_B1) * gv
        v2 = ADAM_B2 * v_ref[...] + (1.0 - ADAM_B2) * jnp.square(gv)
        m_hat = m2 / (1.0 - ADAM_B1 ** ADAM_STEP)
        v_hat = v2 / (1.0 - ADAM_B2 ** ADAM_STEP)
        d_ref[...] = -ADAM_LR * (m_hat / (jnp.sqrt(v_hat) + ADAM_EPS) + ADAM_WD * w_ref[...])
        m2_ref[...] = m2
        v2_ref[...] = v2

    blk = pl.BlockSpec((tr, C), lambda i: (i, 0))
    o = jax.ShapeDtypeStruct((R, C), F32)
    return pl.pallas_call(
        body, name=name, grid=(R // tr,), out_shape=(o, o, o), in_specs=[blk] * 4, out_specs=(blk, blk, blk),
        compiler_params=_cparams("parallel"))(w, g, m, v)


def _fill_own_slot(got, own, chip):
    return {k: lax.dynamic_update_index_in_dim(g, o, chip, 0) for k, g, o in zip(BIG, got, own)}


def _rs_pair_stage(grads, from_sib, core_arr):
    return [_pair_sum(grads[k], r, core_arr, "gradrs_pairsum_" + k) for k, r in zip(BIG, from_sib)]


def _rs_finish(pair, recv, chip, core):
    mine = [_sum_lead(lax.dynamic_update_index_in_dim(r, lax.dynamic_index_in_dim(p, chip, 0, keepdims=False), chip, 0),
                      "gradrs_sum_" + k) for k, r, p in zip(BIG, recv, pair)]
    other = _sib_exchange(mine, "gradrs_sib")
    return {k: jnp.where(core == 0, jnp.concatenate([a, b], axis=0), jnp.concatenate([b, a], axis=0))
            for k, a, b in zip(BIG, mine, other)}


SMALL_LAYER = (("ln1", D), ("ln2", D), ("conv_w", 3 * D), ("ssm_conv_w", 4 * XBC), ("ssm_conv_b", XBC), ("dt_bias", HEADS),
               ("a_log", HEADS), ("d_skip", HEADS), ("ssm_norm_w", D_SSM), ("dmod", 6 * D))
SMALL_ROWS = 136


def _pack_small(layer_parts, fnorm_g, loss):
    vecs = []
    for lp in layer_parts:
        for k, n in SMALL_LAYER:
            vecs.append(lp[k].reshape(n))
    vecs += [fnorm_g.reshape(D), loss.reshape(1)]
    flat = jnp.concatenate(vecs)
    return jnp.pad(flat, (0, SMALL_ROWS * D - flat.shape[0])).reshape(SMALL_ROWS, D)


def _unpack_small(flat2d):
    flat = flat2d.reshape(-1)
    layers, r = [], 0
    for _ in range(DEPTH):
        lp = {}
        for k, n in SMALL_LAYER:
            lp[k] = flat[r:r + n]
            r += n
        layers.append(lp)
    return layers, flat[r:r + D], flat[r + D]


def kernel(x, c, w_ada, b_ada, ln1, ln2, w_in, conv_w, ssm_conv_w, ssm_conv_b, dt_bias, a_log, d_skip, ssm_norm_w, w_conv_out, w_ssm_out, w_o, w_up, w_down, final_norm, loss_target, m_w_ada, m_b_ada, m_ln1, m_ln2, m_w_in, m_conv_w, m_ssm_conv_w, m_ssm_conv_b, m_dt_bias, m_a_log, m_d_skip, m_ssm_norm_w, m_w_conv_out, m_w_ssm_out, m_w_o, m_w_up, m_w_down, m_final_norm, v_w_ada, v_b_ada, v_ln1, v_ln2, v_w_in, v_conv_w, v_ssm_conv_w, v_ssm_conv_b, v_dt_bias, v_a_log, v_d_skip, v_ssm_norm_w, v_w_conv_out, v_w_ssm_out, v_w_o, v_w_up, v_w_down, v_final_norm):
    xi, yi, ci = _my_pos()
    chip = 2 * xi + yi
    dev = 4 * xi + 2 * yi + ci
    core_arr = jnp.reshape(ci, (1,)).astype(jnp.int32)
    h_in, target = x[0], loss_target[0]
    ada_cols = w_ada.shape[2]

    c_all = _all_gather_small(jnp.pad(c, ((0, 7), (0, 0))), "gather_c").reshape(8, 8, D)[:, 0]
    b_sh = lax.dynamic_slice_in_dim(b_ada, chip * ada_cols, ada_cols, axis=1)[:, None, :]
    mod_sh = _ada_fwd(c_all, w_ada, b_sh)
    mod_all = _all_gather_small(mod_sh.reshape(-1, D), "gather_mod").reshape(8, DEPTH, 8, ada_cols)
    mod_mine = lax.dynamic_index_in_dim(mod_all, dev, axis=2, keepdims=False)
    mod = jnp.concatenate([mod_mine[2 * s] for s in range(4)], axis=-1).reshape(DEPTH, 6, D)

    shards = dict(w_in=w_in, w_conv_out=w_conv_out, w_ssm_out=w_ssm_out, w_o=w_o, w_up=w_up, w_down=w_down)
    small = dict(ln1=ln1, ln2=ln2, conv_w=None, ssm_conv_w=None, ssm_conv_b=ssm_conv_b, dt_bias=dt_bias, a_log=a_log,
                 d_skip=d_skip, ssm_norm_w=ssm_norm_w)
    taps = jnp.concatenate([conv_w.reshape(-1), ssm_conv_w.reshape(-1)])
    taps = jnp.pad(taps, (0, 24 * D - taps.shape[0])).reshape(24, D)
    taps_all = _all_gather_small(taps, "gather_taps").reshape(8, 24 * D)
    cw_cols, sw_cols = conv_w.shape[2], ssm_conv_w.shape[2]
    n_cw = DEPTH * 3 * cw_cols
    conv_w_full = jnp.concatenate([taps_all[2 * s, :n_cw].reshape(DEPTH, 3, cw_cols) for s in range(4)], axis=-1)
    ssm_conv_w_full = jnp.concatenate(
        [taps_all[2 * s, n_cw:n_cw + DEPTH * 4 * sw_cols].reshape(DEPTH, 4, sw_cols) for s in range(4)], axis=-1)

    saved, Ws, sps = [], [], []
    h_prev, br_prev, gate_prev = h_in, None, None
    own_bf16 = [[shards[k][l].astype(BF16) for k in BIG] for l in range(DEPTH)]
    gathered = _fill_own_slot(_gather_weights(own_bf16[0], "wgather"), own_bf16[0], chip)
    for l in range(DEPTH):
        W = _prep_layer_weights(gathered)
        p = {k: v[l] for k, v in small.items() if v is not None}
        p["conv_w"], p["ssm_conv_w"] = conv_w_full[l], ssm_conv_w_full[l]
        sp = _prep_small(p)
        mrows = [mod[l, i:i + 1] for i in range(6)]
        nxt = own_bf16[l + 1] if l + 1 < DEPTH else None
        s = _layer_fwd(h_prev, br_prev, gate_prev, mrows, sp, W, nxt)
        if nxt is not None:
            gathered = _fill_own_slot(s.pop("next_gathered"), nxt, chip)
        saved.append(s)
        Ws.append(W)
        sps.append((sp, mrows))
        h_prev, br_prev, gate_prev = s["h1"], s["dn"], mrows[5]

    dh, ddn, head_acc = _loss_head(saved[-1]["h1"], saved[-1]["dn"], sps[-1][1][5], final_norm[None, :], target)
    dgate2 = head_acc[1]
    layer_small = [None] * DEPTH
    big_grads = [None] * DEPTH
    pair_above = None
    for l in reversed(range(DEPTH)):
        sp, mrows = sps[l]
        below = (saved[l - 1]["dn"], sps[l - 1][1][5]) if l > 0 else (None, None)
        big, from_sib, dh, ddn, accs, rs_recv = _layer_bwd(saved[l], dh, ddn, mrows, sp, Ws[l], below[0], below[1],
                                                           pair_above)
        if pair_above is not None:
            big_grads[l + 1] = _rs_finish(pair_above, rs_recv, chip, ci)
        lg = _collect_layer_grads(big, accs)
        lg["dmod"] = jnp.stack(lg["dmod"] + [dgate2])
        if l > 0:
            dgate2 = accs["acc1"][3]
        layer_small[l] = lg
        pair_above = _rs_pair_stage(big, from_sib, core_arr)
    big_grads[0] = _rs_finish(pair_above, _xchip_exchange(pair_above, "gradrs_x"), chip, ci)
    grad_x = dh[None]

    packed = _pack_small(layer_small, head_acc[0], head_acc[2, 0])
    parts = _all_gather_small(packed, "gather_small").reshape(8, SMALL_ROWS, D)
    tot_layers, g_fnorm, loss = _unpack_small(_sum_lead(parts, "sum_small"))
    dmod_all = parts.reshape(8, -1)
    off, per_layer = 0, sum(n for _, n in SMALL_LAYER)
    dmod_off = per_layer - 6 * D
    dmod_dev = jnp.stack([dmod_all[:, l * per_layer + dmod_off:(l + 1) * per_layer] for l in range(DEPTH)], axis=0)
    dmod_cols = lax.dynamic_slice_in_dim(dmod_dev, chip * ada_cols, ada_cols, axis=2)
    g_w_ada = _ada_bwd(jnp.pad(c_all, ((0, 120), (0, 0))), jnp.pad(dmod_cols, ((0, 0), (0, 120), (0, 0))))

    stack = lambda k: jnp.stack([tot_layers[l][k] for l in range(DEPTH)])
    grads = dict(
        w_ada=g_w_ada, b_ada=stack("dmod"), ln1=stack("ln1"), ln2=stack("ln2"),
        w_in=jnp.stack([big_grads[l]["w_in"] for l in range(DEPTH)]),
        conv_w=lax.dynamic_slice_in_dim(stack("conv_w").reshape(DEPTH, 3, D), chip * cw_cols, cw_cols, axis=2),
        ssm_conv_w=lax.dynamic_slice_in_dim(stack("ssm_conv_w").reshape(DEPTH, 4, XBC), chip * sw_cols, sw_cols, axis=2),
        ssm_conv_b=stack("ssm_conv_b"), dt_bias=stack("dt_bias"), a_log=stack("a_log"), d_skip=stack("d_skip"),
        ssm_norm_w=stack("ssm_norm_w"),
        w_conv_out=jnp.stack([big_grads[l]["w_conv_out"] for l in range(DEPTH)]),
        w_ssm_out=jnp.stack([big_grads[l]["w_ssm_out"] for l in range(DEPTH)]),
        w_o=jnp.stack([big_grads[l]["w_o"] for l in range(DEPTH)]),
        w_up=jnp.stack([big_grads[l]["w_up"] for l in range(DEPTH)]),
        w_down=jnp.stack([big_grads[l]["w_down"] for l in range(DEPTH)]),
        final_norm=g_fnorm)
    params = dict(w_ada=w_ada, b_ada=b_ada, ln1=ln1, ln2=ln2, w_in=w_in, conv_w=conv_w, ssm_conv_w=ssm_conv_w,
                  ssm_conv_b=ssm_conv_b, dt_bias=dt_bias, a_log=a_log, d_skip=d_skip, ssm_norm_w=ssm_norm_w,
                  w_conv_out=w_conv_out, w_ssm_out=w_ssm_out, w_o=w_o, w_up=w_up, w_down=w_down, final_norm=final_norm)
    ms = dict(w_ada=m_w_ada, b_ada=m_b_ada, ln1=m_ln1, ln2=m_ln2, w_in=m_w_in, conv_w=m_conv_w, ssm_conv_w=m_ssm_conv_w,
              ssm_conv_b=m_ssm_conv_b, dt_bias=m_dt_bias, a_log=m_a_log, d_skip=m_d_skip, ssm_norm_w=m_ssm_norm_w,
              w_conv_out=m_w_conv_out, w_ssm_out=m_w_ssm_out, w_o=m_w_o, w_up=m_w_up, w_down=m_w_down,
              final_norm=m_final_norm)
    vs = dict(w_ada=v_w_ada, b_ada=v_b_ada, ln1=v_ln1, ln2=v_ln2, w_in=v_w_in, conv_w=v_conv_w, ssm_conv_w=v_ssm_conv_w,
              ssm_conv_b=v_ssm_conv_b, dt_bias=v_dt_bias, a_log=v_a_log, d_skip=v_d_skip, ssm_norm_w=v_ssm_norm_w,
              w_conv_out=v_w_conv_out, w_ssm_out=v_w_ssm_out, w_o=v_w_o, w_up=v_w_up, w_down=v_w_down,
              final_norm=v_final_norm)
    names = list(params)
    deltas, new_m, new_v = {}, {}, {}
    for k in names:
        shp = params[k].shape
        two_d = (-1, shp[-1]) if len(shp) > 1 else (1, shp[0])
        gk = grads[k].reshape(shp)
        grads[k] = gk
        d_, m_, v_ = _adamw(params[k].reshape(two_d), gk.reshape(two_d), ms[k].reshape(two_d), vs[k].reshape(two_d),
                            "adamw_" + k)
        deltas[k], new_m[k], new_v[k] = d_.reshape(shp), m_.reshape(shp), v_.reshape(shp)
    return (loss, grad_x, *[grads[k] for k in names], *[deltas[k] for k in names], *[new_m[k] for k in names],
            *[new_v[k] for k in names])
```
